```python
import jax
import jax.numpy as jnp
from jax import lax
import numpy as np

D_MODEL = 2048
BATCH = 8
SEQ = 4096
DEPTH = 1

D_MIX = D_MODEL
V_HEAD_DIM = 128
MLA_WIDTH = D_MIX // 2
MLA_HEADS = MLA_WIDTH // V_HEAD_DIM
QK_NOPE_DIM = 128
QK_ROPE_DIM = 64
QK_HEAD_DIM = QK_NOPE_DIM + QK_ROPE_DIM
Q_LORA_RANK = 512
KV_LORA_RANK = 512
ROPE_THETA = 10000.0
Q_BLOCK = 128
SSD_WIDTH = D_MIX - MLA_WIDTH
SSD_HEAD_DIM = 64
SSD_HEADS = SSD_WIDTH // SSD_HEAD_DIM
SSD_GROUPS = 2
SSD_HEADS_PER_GROUP = SSD_HEADS // SSD_GROUPS
SSD_STATE = 128
SSD_CONV = 4
SSD_CHUNK = 128
SSD_CONV_DIM = SSD_WIDTH + 2 * SSD_GROUPS * SSD_STATE
D_FF = -(-8 * D_MODEL // (3 * 256)) * 256
IN_SIZES = (Q_LORA_RANK, KV_LORA_RANK, QK_ROPE_DIM, SSD_WIDTH, SSD_CONV_DIM, SSD_HEADS)
D_IN = Q_LORA_RANK + KV_LORA_RANK + QK_ROPE_DIM + SSD_WIDTH + SSD_CONV_DIM + SSD_HEADS
EPS = 1e-6

kernel_name = "hymba_mla_ssd_sandwich_layer"


def rms_norm(t, w):
    tf = t.astype(jnp.float32)
    y = tf * lax.rsqrt(jnp.mean(tf * tf, axis=-1, keepdims=True) + EPS)
    return (y * w.astype(jnp.float32)).astype(t.dtype)


def split_cols(t, sizes):
    offsets = np.cumsum(np.array(sizes))[:-1].tolist()
    return jnp.split(t, offsets, axis=-1)


def rope_tables(positions):
    inv_freq = ROPE_THETA ** (-jnp.arange(0, QK_ROPE_DIM, 2, dtype=jnp.float32) / QK_ROPE_DIM)
    ang = positions.astype(jnp.float32)[..., None] * inv_freq
    return jnp.cos(ang), jnp.sin(ang)


def apply_rope(t, cos, sin):
    t1, t2 = jnp.split(t.astype(jnp.float32), 2, axis=-1)
    return jnp.concatenate([t1 * cos - t2 * sin, t2 * cos + t1 * sin], axis=-1).astype(t.dtype)


def mla_group(c_q, c_kv, k_rope, cos, sin, q_norm_w, w_uq, kv_norm_w, w_ukv):
    b, s, _ = c_q.shape
    q = (rms_norm(c_q, q_norm_w) @ w_uq).reshape(b, s, MLA_HEADS, QK_HEAD_DIM)
    q_nope, q_rope = q[..., :QK_NOPE_DIM], q[..., QK_NOPE_DIM:]
    kv = (rms_norm(c_kv, kv_norm_w) @ w_ukv).reshape(b, s, MLA_HEADS, QK_NOPE_DIM + V_HEAD_DIM)
    k_nope, v = kv[..., :QK_NOPE_DIM], kv[..., QK_NOPE_DIM:]
    q_rope = apply_rope(q_rope, cos[:, :, None, :], sin[:, :, None, :])
    k_rope = apply_rope(k_rope, cos, sin)
    scale = QK_HEAD_DIM ** -0.5
    n_blk = s // Q_BLOCK
    qn_blocks = jnp.moveaxis(q_nope.reshape(b, n_blk, Q_BLOCK, MLA_HEADS, QK_NOPE_DIM), 1, 0)
    qr_blocks = jnp.moveaxis(q_rope.reshape(b, n_blk, Q_BLOCK, MLA_HEADS, QK_ROPE_DIM), 1, 0)
    key_idx = jnp.arange(s)

    def attend(args):
        blk, qn, qr = args
        sc = (jnp.einsum('bqhd,bkhd->bhqk', qn, k_nope, preferred_element_type=jnp.float32)
              + jnp.einsum('bqhr,bkr->bhqk', qr, k_rope, preferred_element_type=jnp.float32)) * scale
        q_idx = blk * Q_BLOCK + jnp.arange(Q_BLOCK)
        causal = key_idx[None, :] <= q_idx[:, None]
        p = jax.nn.softmax(jnp.where(causal, sc, -jnp.inf), axis=-1).astype(v.dtype)
        return jnp.einsum('bhqk,bkhd->bqhd', p, v)

    o = lax.map(attend, (jnp.arange(n_blk), qn_blocks, qr_blocks))
    return jnp.moveaxis(o, 0, 1).reshape(b, s, MLA_HEADS * V_HEAD_DIM)


def causal_depthwise_conv(t, w, bias):
    y = lax.conv_general_dilated(t, w[:, None, :], window_strides=(1,), padding=[(SSD_CONV - 1, 0)],
                                 dimension_numbers=('NWC', 'WIO', 'NWC'), feature_group_count=t.shape[-1])
    return y + bias


def ssd_group(z, xbc, dt_raw, conv_w, conv_b, dt_bias, a_log, d_skip, norm_w):
    b, s, _ = z.shape
    G, E, P, N, T = SSD_GROUPS, SSD_HEADS_PER_GROUP, SSD_HEAD_DIM, SSD_STATE, SSD_CHUNK
    c = s // T
    xbc = jax.nn.silu(causal_depthwise_conv(xbc, conv_w, conv_b))
    xs, bm, cm = split_cols(xbc, (SSD_WIDTH, G * N, G * N))
    dt = jax.nn.softplus(dt_raw.astype(jnp.float32) + dt_bias.astype(jnp.float32))
    a_neg = -jnp.exp(a_log.astype(jnp.float32)).reshape(G, E)
    x = xs.astype(jnp.float32).reshape(b, c, T, G, E, P)
    dt_c = dt.reshape(b, c, T, G, E)
    bc = bm.astype(jnp.float32).reshape(b, c, T, G, N)
    cc = cm.astype(jnp.float32).reshape(b, c, T, G, N)
    xdt = x * dt_c[..., None]
    a_cum = jnp.cumsum(jnp.transpose(dt_c * a_neg, (0, 1, 3, 4, 2)), axis=-1)
    seg = a_cum[..., :, None] - a_cum[..., None, :]
    tri = jnp.tril(jnp.ones((T, T), dtype=bool))
    decay = jnp.exp(jnp.where(tri, seg, -jnp.inf))
    cb = jnp.einsum('bclgn,bcsgn->bcgls', cc, bc)
    y_diag = jnp.einsum('bcgels,bcsgep->bclgep', cb[:, :, :, None] * decay, xdt)
    decay_states = jnp.exp(a_cum[..., -1:] - a_cum)
    states = jnp.einsum('bcsgn,bcsgep->bcgepn', bc, xdt * jnp.transpose(decay_states, (0, 1, 4, 2, 3))[..., None])
    chunk_decay = jnp.exp(a_cum[..., -1])

    def step(h, inp):
        st, dec = inp
        return h * dec[..., None, None] + st, h

    h0 = jnp.zeros((b, G, E, P, N), jnp.float32)
    _, prev = lax.scan(step, h0, (jnp.moveaxis(states, 1, 0), jnp.moveaxis(chunk_decay, 1, 0)))
    prev = jnp.moveaxis(prev, 0, 1)
    y_off = jnp.einsum('bclgn,bcgepn->bclgep', cc, prev) * jnp.transpose(jnp.exp(a_cum), (0, 1, 4, 2, 3))[..., None]
    y = (y_diag + y_off + x * d_skip.astype(jnp.float32).reshape(G, E, 1)).reshape(b, s, SSD_WIDTH)
    g = (y * jax.nn.silu(z.astype(jnp.float32))).reshape(b, s, G, SSD_WIDTH // G)
    g = g * lax.rsqrt(jnp.mean(g * g, axis=-1, keepdims=True) + EPS)
    return (g.reshape(b, s, SSD_WIDTH) * norm_w.astype(jnp.float32)).astype(z.dtype)


def setup_inputs(seed: int = 0) -> dict:
    key = jax.random.key(seed)
    ks = jax.random.split(key, 24)
    f32 = jnp.float32
    L = DEPTH

    def dense(k, fan_in, fan_out):
        return jax.random.normal(k, (L, fan_in, fan_out), f32) * fan_in ** -0.5

    def gain(k, n):
        return 1.0 + 0.02 * jax.random.normal(k, (L, n), f32)

    x = jax.random.normal(ks[0], (BATCH, SEQ, D_MODEL), f32)
    positions = jnp.arange(SEQ, dtype=jnp.int32)[None, :] + jax.random.randint(ks[1], (BATCH, 1), 0, SEQ, dtype=jnp.int32)
    dt0 = jnp.exp(jax.random.uniform(ks[2], (L, SSD_HEADS), f32, float(np.log(1e-3)), float(np.log(1e-1))))
    dt_bias = dt0 + jnp.log(-jnp.expm1(-dt0))
    a_log = jnp.log(jax.random.uniform(ks[3], (L, SSD_HEADS), f32, 1.0, 16.0))
    return {
        "x": x,
        "positions": positions,
        "w_in": dense(ks[4], D_MODEL, D_IN),
        "q_norm_w": gain(ks[5], Q_LORA_RANK),
        "w_uq": dense(ks[6], Q_LORA_RANK, MLA_HEADS * QK_HEAD_DIM),
        "kv_norm_w": gain(ks[7], KV_LORA_RANK),
        "w_ukv": dense(ks[8], KV_LORA_RANK, MLA_HEADS * (QK_NOPE_DIM + V_HEAD_DIM)),
        "conv_w": jax.random.normal(ks[9], (L, SSD_CONV, SSD_CONV_DIM), f32) * SSD_CONV ** -0.5,
        "conv_b": 0.02 * jax.random.normal(ks[10], (L, SSD_CONV_DIM), f32),
        "dt_bias": dt_bias,
        "a_log": a_log,
        "d_skip": gain(ks[11], SSD_HEADS),
        "ssd_norm_w": gain(ks[12], SSD_WIDTH),
        "attn_out_norm_w": gain(ks[13], MLA_WIDTH),
        "w_out": dense(ks[14], D_MIX, D_MODEL),
        "pre_mix_norm_w": gain(ks[15], D_MODEL),
        "post_mix_norm_w": gain(ks[16], D_MODEL),
        "pre_ffn_norm_w": gain(ks[17], D_MODEL),
        "post_ffn_norm_w": gain(ks[18], D_MODEL),
        "w_gate": dense(ks[19], D_MODEL, D_FF),
        "w_up": dense(ks[20], D_MODEL, D_FF),
        "w_down": dense(ks[21], D_FF, D_MODEL),
    }


def reference(x, positions, w_in, q_norm_w, w_uq, kv_norm_w, w_ukv, conv_w, conv_b, dt_bias, a_log,
              d_skip, ssd_norm_w, attn_out_norm_w, w_out, pre_mix_norm_w, post_mix_norm_w,
              pre_ffn_norm_w, post_ffn_norm_w, w_gate, w_up, w_down):
    cos, sin = rope_tables(positions)
    h = x
    for l in range(DEPTH):
        u = rms_norm(h, pre_mix_norm_w[l])
        c_q, c_kv, k_rope, z, xbc, dt_raw = split_cols(u @ w_in[l], IN_SIZES)
        attn = rms_norm(mla_group(c_q, c_kv, k_rope, cos, sin, q_norm_w[l], w_uq[l], kv_norm_w[l], w_ukv[l]),
                        attn_out_norm_w[l])
        ssm = ssd_group(z, xbc, dt_raw, conv_w[l], conv_b[l], dt_bias[l], a_log[l], d_skip[l], ssd_norm_w[l])
        mix = jnp.concatenate([attn, ssm], axis=-1) @ w_out[l]
        h = h + rms_norm(mix, post_mix_norm_w[l])
        v = rms_norm(h, pre_ffn_norm_w[l])
        ffn = (jax.nn.silu(v @ w_gate[l]) * (v @ w_up[l])) @ w_down[l]
        h = h + rms_norm(ffn, post_ffn_norm_w[l])
    return h
```

```python
import functools

import numpy as np
import jax
import jax.numpy as jnp
from jax import lax
from jax.experimental import pallas as pl
from jax.experimental.pallas import tpu as pltpu

F32 = jnp.float32
BF16 = jnp.bfloat16

V_HEAD_DIM = 128
QK_NOPE_DIM = 128
QK_ROPE_DIM = 64
QK_HEAD_DIM = QK_NOPE_DIM + QK_ROPE_DIM
Q_LORA_RANK = 512
KV_LORA_RANK = 512
ROPE_THETA = 10000.0
SSD_HEAD_DIM = 64
SSD_GROUPS = 2
SSD_STATE = 128
SSD_CONV = 4
SSD_CHUNK = 128
EPS = 1e-6

LANES = 128
SUBLANES = 8
VMEM_LIMIT_BYTES = 56 * 1024 * 1024

QK_PAD = 2 * LANES
ROPE_HALF = QK_ROPE_DIM // 2


def _rms(t, w):
    return t * lax.rsqrt(jnp.mean(t * t, axis=-1, keepdims=True) + EPS) * w


def _sigmoid(t):
    return 1.0 / (1.0 + jnp.exp(-t))


def _rope(t, cos, sin_signed):
    return t * cos + pltpu.roll(t, LANES // 2, axis=1) * sin_signed


def _in_proj_kernel(x_ref, pos_ref, wpre_ref, win_ref, qnw_ref, wuq_ref, kvnw_ref, wukv_ref,
                    freq_ref, sgn_ref, q_ref, kv_ref, kr_ref, z_ref, xbc_ref, dt_ref,
                    *, n_heads, d_z, d_xbc, q_scale):
    u = _rms(x_ref[...], wpre_ref[...]).astype(BF16)
    ang = pos_ref[...] * freq_ref[...]
    cos = jnp.cos(ang)
    sin = jnp.sin(ang) * sgn_ref[...]

    o0 = 0
    o1 = o0 + Q_LORA_RANK + KV_LORA_RANK
    o2 = o1 + d_z
    o3 = o2 + d_xbc
    cqkv = jnp.dot(u, win_ref[:, o0:o1], preferred_element_type=F32)
    cqn = _rms(cqkv[:, :Q_LORA_RANK], qnw_ref[...]).astype(BF16)
    ckvn = _rms(cqkv[:, Q_LORA_RANK:], kvnw_ref[...]).astype(BF16)

    q = jnp.dot(cqn, wuq_ref[...], preferred_element_type=F32)
    cos_q = cos * q_scale
    sin_q = sin * q_scale
    for h in range(n_heads):
        base = h * QK_PAD
        q_ref[:, base:base + LANES] = (q[:, base:base + LANES] * q_scale).astype(BF16)
        q_ref[:, base + LANES:base + QK_PAD] = _rope(q[:, base + LANES:base + QK_PAD], cos_q, sin_q).astype(BF16)

    kv_ref[...] = jnp.dot(ckvn, wukv_ref[...], preferred_element_type=F32).astype(BF16)
    z_ref[...] = jnp.dot(u, win_ref[:, o1:o2], preferred_element_type=F32).astype(BF16)
    xbc_ref[...] = jnp.dot(u, win_ref[:, o2:o3], preferred_element_type=F32).astype(BF16)
    misc = jnp.dot(u, win_ref[:, o3:o3 + 2 * LANES], preferred_element_type=F32)
    kr_ref[...] = _rope(misc[:, :LANES], cos, sin).astype(BF16)
    dt_ref[...] = misc[:, LANES:]


def _mla_kernel(q_ref, kv_ref, kr_ref, o_ref, kc_ref, m_ref, l_ref, acc_ref, *, tile):
    qi = pl.program_id(2)

    @pl.when(qi == 0)
    def _():
        kc_ref[:, :LANES] = kv_ref[0, :, :LANES]
        kc_ref[:, LANES:] = kr_ref[0]

    q = q_ref[0]
    m_ref[...] = jnp.full_like(m_ref, -jnp.inf)
    l_ref[...] = jnp.zeros_like(l_ref)
    acc_ref[...] = jnp.zeros_like(acc_ref)

    def update(j, masked):
        rows = pl.ds(pl.multiple_of(j * tile, tile), tile)
        s = lax.dot_general(q, kc_ref[rows, :], (((1,), (1,)), ((), ())), preferred_element_type=F32)
        if masked:
            r = lax.broadcasted_iota(jnp.int32, s.shape, 0)
            c = lax.broadcasted_iota(jnp.int32, s.shape, 1)
            s = jnp.where(c <= r, s, -jnp.inf)
        m_prev = m_ref[...]
        m_new = jnp.maximum(m_prev, jnp.max(s, axis=-1, keepdims=True))
        alpha = jnp.exp(m_prev - m_new)
        p = jnp.exp(s - m_new)
        l_ref[...] = alpha * l_ref[...] + jnp.sum(p, axis=-1, keepdims=True)
        acc_ref[...] = alpha * acc_ref[...] + jnp.dot(p.astype(BF16), kv_ref[0, rows, LANES:],
                                                      preferred_element_type=F32)
        m_ref[...] = m_new

    def body(j, carry):
        update(j, False)
        return carry

    lax.fori_loop(0, qi, body, 0)
    update(qi, True)
    o_ref[0] = (acc_ref[...] / l_ref[...]).astype(o_ref.dtype)


def _split2(t):
    hi = t.astype(BF16)
    return hi, (t - hi.astype(F32)).astype(BF16)


def _ssd_kernel(xbc_ref, z_ref, dt_ref, convw_ref, convb_ref, dtb_ref, aneg_ref, dskip_ref, nw_ref,
                tri3_ref, e2_ref, o_ref, ext_ref, state_ref, *, d_inner, n_heads):
    T, N, P, G = SSD_CHUNK, SSD_STATE, SSD_HEAD_DIM, SSD_GROUPS
    gw = d_inner // G
    halo = SUBLANES

    @pl.when(pl.program_id(1) == 0)
    def _():
        ext_ref[0:halo, :] = jnp.zeros((halo, ext_ref.shape[1]), F32)
        state_ref[...] = jnp.zeros_like(state_ref)

    ext_ref[halo:halo + T, :] = xbc_ref[0].astype(F32)
    conv = convb_ref[...] + convw_ref[0:1, :] * ext_ref[halo - 3:halo - 3 + T, :]
    for k in range(1, SSD_CONV):
        conv = conv + convw_ref[k:k + 1, :] * ext_ref[halo - 3 + k:halo - 3 + k + T, :]
    ext_ref[0:halo, :] = ext_ref[T:T + halo, :]
    xbc = conv * _sigmoid(conv)
    xs = xbc[:, :d_inner]
    bm = xbc[:, d_inner:d_inner + G * N]
    cm = xbc[:, d_inner + G * N:]

    dt_in = dt_ref[0] + dtb_ref[...]
    dt = jnp.maximum(dt_in, 0.0) + jnp.log1p(jnp.exp(-jnp.abs(dt_in)))
    a = dt * aneg_ref[...]
    a_hi = a.astype(BF16)
    a_r = a - a_hi.astype(F32)
    a_mid = a_r.astype(BF16)
    a_lo = (a_r - a_mid.astype(F32)).astype(BF16)
    a_cum = jnp.dot(tri3_ref[...], jnp.concatenate([a_hi, a_mid, a_lo], axis=0), preferred_element_type=F32)
    a_cum_t = a_cum.T
    a_last = a_cum[T - 1:T, :]
    ea = jnp.exp(a_cum)
    ds = jnp.exp(a_last - a_cum)

    v_hi, v_lo = _split2(jnp.concatenate([dt, ea, ds], axis=0))
    v_exp = jnp.dot(jnp.concatenate([v_hi, v_lo], axis=1), e2_ref[...], preferred_element_type=F32)
    dt_x, ea_x, ds_x = v_exp[0:T], v_exp[T:2 * T], v_exp[2 * T:3 * T]

    xdt = xs * dt_x
    xdt_b = xdt.astype(BF16)
    xw_b = (xdt * ds_x).astype(BF16)

    row = lax.broadcasted_iota(jnp.int32, (T, T), 0)
    col = lax.broadcasted_iota(jnp.int32, (T, T), 1)
    tri = col <= row
    lane = lax.broadcasted_iota(jnp.int32, (T, 2 * P), 1)
    heads_per_group = n_heads // G

    y_parts = []
    for g in range(G):
        bm_g = bm[:, g * N:(g + 1) * N]
        cm_g = cm[:, g * N:(g + 1) * N].astype(BF16)
        cb = lax.dot_general(cm_g, bm_g.astype(BF16), (((1,), (1,)), ((), ())), preferred_element_type=F32)
        for j in range(heads_per_group // 2):
            w_pair = []
            for h in (g * heads_per_group + 2 * j, g * heads_per_group + 2 * j + 1):
                seg = a_cum[:, h:h + 1] - a_cum_t[h:h + 1, :]
                w_pair.append((cb * jnp.exp(jnp.where(tri, seg, -jnp.inf))).astype(BF16))
            c0 = g * gw + 2 * j * P
            xp = xdt_b[:, c0:c0 + 2 * P]
            zero = jnp.zeros_like(xp)
            x_pair = jnp.concatenate([jnp.where(lane < P, xp, zero), jnp.where(lane >= P, xp, zero)], axis=0)
            y_parts.append(jnp.dot(jnp.concatenate(w_pair, axis=1), x_pair, preferred_element_type=F32))
    y_diag = jnp.concatenate(y_parts, axis=1)

    y_off_parts = []
    for g in range(G):
        cols = slice(g * gw, (g + 1) * gw)
        bm_t = bm[:, g * N:(g + 1) * N].T.astype(BF16)
        cm_g = cm[:, g * N:(g + 1) * N].astype(BF16)
        prev = state_ref[g]
        y_off_parts.append(jnp.dot(cm_g, prev.astype(BF16), preferred_element_type=F32))
        new = jnp.dot(bm_t, xw_b[:, cols], preferred_element_type=F32)
        state_ref[g] = prev * ea_x[T - 1:T, cols] + new
    y_off = jnp.concatenate(y_off_parts, axis=1) * ea_x

    y = y_diag + y_off + xs * dskip_ref[...]
    zf = z_ref[0].astype(F32)
    gated = y * (zf * _sigmoid(zf))
    outs = []
    for g in range(G):
        gg = gated[:, g * gw:(g + 1) * gw]
        outs.append(gg * lax.rsqrt(jnp.mean(gg * gg, axis=-1, keepdims=True) + EPS))
    o_ref[0] = (jnp.concatenate(outs, axis=1) * nw_ref[...]).astype(o_ref.dtype)


def _out_proj_kernel(x_ref, attn_ref, ssm_ref, anw_ref, woa_ref, wos_ref, postw_ref, h_ref):
    attn_n = _rms(attn_ref[...].astype(F32), anw_ref[...]).astype(BF16)
    mix = jnp.dot(attn_n, woa_ref[...], preferred_element_type=F32)
    mix = mix + jnp.dot(ssm_ref[...], wos_ref[...], preferred_element_type=F32)
    h_ref[...] = x_ref[...] + _rms(mix, postw_ref[...])


def _ffn_kernel(h_ref, prew_ref, wg_ref, wu_ref, wd_ref, postw_ref, o_ref, v_ref):
    f = pl.program_id(1)

    @pl.when(f == 0)
    def _():
        v_ref[...] = _rms(h_ref[...], prew_ref[...]).astype(BF16)

    v = v_ref[...]
    gate = jnp.dot(v, wg_ref[...], preferred_element_type=F32)
    up = jnp.dot(v, wu_ref[...], preferred_element_type=F32)
    act = (gate * _sigmoid(gate) * up).astype(BF16)
    part = jnp.dot(act, wd_ref[...], preferred_element_type=F32)

    @pl.when(f == 0)
    def _():
        o_ref[...] = part

    @pl.when(f > 0)
    def _():
        o_ref[...] += part

    @pl.when(f == pl.num_programs(1) - 1)
    def _():
        o_ref[...] = h_ref[...] + _rms(o_ref[...], postw_ref[...])


def _tile(n, want):
    t = min(n, want)
    assert n % t == 0, (n, t)
    return t


def _resident(shape):
    return pl.BlockSpec(shape, lambda *_: (0,) * len(shape), pipeline_mode=pl.Buffered(1))


def _row(w):
    return w.reshape(1, -1).astype(F32)


def _layer(h, pos_col, p):
    b, s, d_model = h.shape
    m = b * s
    n_mla_heads = p["w_uq"].shape[1] // QK_HEAD_DIM
    mla_width = n_mla_heads * V_HEAD_DIM
    d_inner = p["ssd_norm_w"].shape[0]
    n_ssd_heads = p["dt_bias"].shape[0]
    d_xbc = d_inner + 2 * SSD_GROUPS * SSD_STATE
    d_ff = p["w_gate"].shape[1]
    assert n_ssd_heads * SSD_HEAD_DIM == d_inner and n_ssd_heads <= LANES
    assert s % SSD_CHUNK == 0

    w_in = p["w_in"]
    offs = np.cumsum([0, Q_LORA_RANK, KV_LORA_RANK, QK_ROPE_DIM, d_inner, d_xbc, n_ssd_heads])
    w_cq, w_ckv, w_kr, w_z, w_xbc, w_dt = [w_in[:, offs[i]:offs[i + 1]] for i in range(6)]
    zpad = lambda n: jnp.zeros((d_model, n), w_in.dtype)
    w_in_r = jnp.concatenate([
        w_cq, w_ckv, w_z, w_xbc,
        w_kr[:, :ROPE_HALF], zpad(ROPE_HALF), w_kr[:, ROPE_HALF:], zpad(ROPE_HALF),
        w_dt, zpad(LANES - n_ssd_heads)], axis=1).astype(BF16)
    d_in_r = w_in_r.shape[1]

    wq3 = p["w_uq"].reshape(Q_LORA_RANK, n_mla_heads, QK_HEAD_DIM)
    zq = jnp.zeros((Q_LORA_RANK, n_mla_heads, ROPE_HALF), wq3.dtype)
    w_uq_r = jnp.concatenate([
        wq3[..., :QK_NOPE_DIM], wq3[..., QK_NOPE_DIM:QK_NOPE_DIM + ROPE_HALF], zq,
        wq3[..., QK_NOPE_DIM + ROPE_HALF:], zq], axis=-1).reshape(Q_LORA_RANK, n_mla_heads * QK_PAD).astype(BF16)
    w_ukv = p["w_ukv"].astype(BF16)

    inv_freq = ROPE_THETA ** (-jnp.arange(0, QK_ROPE_DIM, 2, dtype=F32) / QK_ROPE_DIM)
    z32 = jnp.zeros((ROPE_HALF,), F32)
    o32 = jnp.ones((ROPE_HALF,), F32)
    freq_tab = jnp.concatenate([inv_freq, z32, inv_freq, z32]).reshape(1, LANES)
    sgn_tab = jnp.concatenate([-o32, z32, o32, z32]).reshape(1, LANES)

    tm = _tile(m, 256)
    row_spec = lambda n: pl.BlockSpec((tm, n), lambda i: (i, 0))
    q, kv, kr, z, xbc, dt_raw = pl.pallas_call(
        functools.partial(_in_proj_kernel, n_heads=n_mla_heads, d_z=d_inner, d_xbc=d_xbc,
                          q_scale=float(QK_HEAD_DIM) ** -0.5),
        grid=(m // tm,),
        in_specs=[row_spec(d_model), row_spec(1), _resident((1, d_model)), _resident((d_model, d_in_r)),
                  _resident((1, Q_LORA_RANK)), _resident((Q_LORA_RANK, n_mla_heads * QK_PAD)),
                  _resident((1, KV_LORA_RANK)), _resident((KV_LORA_RANK, w_ukv.shape[1])),
                  _resident((1, LANES)), _resident((1, LANES))],
        out_specs=[row_spec(n_mla_heads * QK_PAD), row_spec(w_ukv.shape[1]), row_spec(LANES),
                   row_spec(d_inner), row_spec(d_xbc), row_spec(LANES)],
        out_shape=[jax.ShapeDtypeStruct((m, n_mla_heads * QK_PAD), BF16),
                   jax.ShapeDtypeStruct((m, w_ukv.shape[1]), BF16),
                   jax.ShapeDtypeStruct((m, LANES), BF16),
                   jax.ShapeDtypeStruct((m, d_inner), BF16),
                   jax.ShapeDtypeStruct((m, d_xbc), BF16),
                   jax.ShapeDtypeStruct((m, LANES), F32)],
        compiler_params=pltpu.CompilerParams(dimension_semantics=("arbitrary",),
                                             vmem_limit_bytes=VMEM_LIMIT_BYTES),
        name="in_proj",
    )(h.reshape(m, d_model), pos_col, _row(p["pre_mix_norm_w"]), w_in_r, _row(p["q_norm_w"]), w_uq_r,
      _row(p["kv_norm_w"]), w_ukv, freq_tab, sgn_tab)

    ta = _tile(s, 512)
    attn = pl.pallas_call(
        functools.partial(_mla_kernel, tile=ta),
        grid=(b, n_mla_heads, s // ta),
        in_specs=[pl.BlockSpec((1, ta, QK_PAD), lambda bi, hi, qi: (bi, qi, hi)),
                  pl.BlockSpec((1, s, 2 * LANES), lambda bi, hi, qi: (bi, 0, hi)),
                  pl.BlockSpec((1, s, LANES), lambda bi, hi, qi: (bi, 0, 0))],
        out_specs=pl.BlockSpec((1, ta, V_HEAD_DIM), lambda bi, hi, qi: (bi, qi, hi)),
        out_shape=jax.ShapeDtypeStruct((b, s, mla_width), BF16),
        scratch_shapes=[pltpu.VMEM((s, QK_PAD), BF16), pltpu.VMEM((ta, 1), F32), pltpu.VMEM((ta, 1), F32),
                        pltpu.VMEM((ta, V_HEAD_DIM), F32)],
        compiler_params=pltpu.CompilerParams(dimension_semantics=("arbitrary", "arbitrary", "arbitrary"),
                                             vmem_limit_bytes=VMEM_LIMIT_BYTES),
        name="mla",
    )(q.reshape(b, s, -1), kv.reshape(b, s, -1), kr.reshape(b, s, LANES))

    T = SSD_CHUNK
    lane_pad = lambda v: jnp.pad(v.astype(F32), (0, LANES - n_ssd_heads)).reshape(1, LANES)
    a_neg = lane_pad(-jnp.exp(p["a_log"].astype(F32)))
    dt_bias = lane_pad(p["dt_bias"])
    d_skip_x = jnp.repeat(p["d_skip"].astype(F32), SSD_HEAD_DIM).reshape(1, d_inner)
    tri = jnp.tril(jnp.ones((T, T), BF16))
    tri3 = jnp.concatenate([tri, tri, tri], axis=1)
    expand = (jnp.arange(LANES)[:, None] == (jnp.arange(d_inner)[None, :] // SSD_HEAD_DIM)).astype(BF16)
    expand2 = jnp.concatenate([expand, expand], axis=0)
    chunk_spec = lambda n: pl.BlockSpec((1, T, n), lambda bi, ci: (bi, ci, 0))
    ssm = pl.pallas_call(
        functools.partial(_ssd_kernel, d_inner=d_inner, n_heads=n_ssd_heads),
        grid=(b, s // T),
        in_specs=[chunk_spec(d_xbc), chunk_spec(d_inner), chunk_spec(LANES),
                  _resident((SSD_CONV, d_xbc)), _resident((1, d_xbc)), _resident((1, LANES)),
                  _resident((1, LANES)), _resident((1, d_inner)), _resident((1, d_inner)),
                  _resident((T, 3 * T)), _resident((2 * LANES, d_inner))],
        out_specs=chunk_spec(d_inner),
        out_shape=jax.ShapeDtypeStruct((b, s, d_inner), BF16),
        scratch_shapes=[pltpu.VMEM((T + 2 * SUBLANES, d_xbc), F32),
                        pltpu.VMEM((SSD_GROUPS, SSD_STATE, d_inner // SSD_GROUPS), F32)],
        compiler_params=pltpu.CompilerParams(dimension_semantics=("arbitrary", "arbitrary"),
                                             vmem_limit_bytes=VMEM_LIMIT_BYTES),
        name="ssd",
    )(xbc.reshape(b, s, d_xbc), z.reshape(b, s, d_inner), dt_raw.reshape(b, s, LANES),
      p["conv_w"].astype(F32), _row(p["conv_b"]), dt_bias, a_neg, d_skip_x, _row(p["ssd_norm_w"]),
      tri3, expand2)

    w_out = p["w_out"].astype(BF16)
    to = _tile(m, 512)
    orow = lambda n: pl.BlockSpec((to, n), lambda i: (i, 0))
    h1 = pl.pallas_call(
        _out_proj_kernel,
        grid=(m // to,),
        in_specs=[orow(d_model), orow(mla_width), orow(d_inner), _resident((1, mla_width)),
                  _resident((mla_width, d_model)), _resident((d_inner, d_model)), _resident((1, d_model))],
        out_specs=orow(d_model),
        out_shape=jax.ShapeDtypeStruct((m, d_model), F32),
        compiler_params=pltpu.CompilerParams(dimension_semantics=("arbitrary",),
                                             vmem_limit_bytes=VMEM_LIMIT_BYTES),
        name="out_proj",
    )(h.reshape(m, d_model), attn.reshape(m, mla_width), ssm.reshape(m, d_inner), _row(p["attn_out_norm_w"]),
      w_out[:mla_width], w_out[mla_width:], _row(p["post_mix_norm_w"]))

    tf_m = _tile(m, 512)
    tf_f = _tile(d_ff, 512)
    out = pl.pallas_call(
        _ffn_kernel,
        grid=(m // tf_m, d_ff // tf_f),
        in_specs=[pl.BlockSpec((tf_m, d_model), lambda i, f: (i, 0)),
                  _resident((1, d_model)),
                  pl.BlockSpec((d_model, tf_f), lambda i, f: (0, f)),
                  pl.BlockSpec((d_model, tf_f), lambda i, f: (0, f)),
                  pl.BlockSpec((tf_f, d_model), lambda i, f: (f, 0)),
                  _resident((1, d_model))],
        out_specs=pl.BlockSpec((tf_m, d_model), lambda i, f: (i, 0)),
        out_shape=jax.ShapeDtypeStruct((m, d_model), F32),
        scratch_shapes=[pltpu.VMEM((tf_m, d_model), BF16)],
        compiler_params=pltpu.CompilerParams(dimension_semantics=("arbitrary", "arbitrary"),
                                             vmem_limit_bytes=VMEM_LIMIT_BYTES),
        name="ffn",
    )(h1, _row(p["pre_ffn_norm_w"]), p["w_gate"].astype(BF16), p["w_up"].astype(BF16),
      p["w_down"].astype(BF16), _row(p["post_ffn_norm_w"]))
    return out.reshape(b, s, d_model)


def kernel(x, positions, w_in, q_norm_w, w_uq, kv_norm_w, w_ukv, conv_w, conv_b, dt_bias, a_log, d_skip,
           ssd_norm_w, attn_out_norm_w, w_out, pre_mix_norm_w, post_mix_norm_w, pre_ffn_norm_w,
           post_ffn_norm_w, w_gate, w_up, w_down):
    stacked = dict(w_in=w_in, q_norm_w=q_norm_w, w_uq=w_uq, kv_norm_w=kv_norm_w, w_ukv=w_ukv, conv_w=conv_w,
                   conv_b=conv_b, dt_bias=dt_bias, a_log=a_log, d_skip=d_skip, ssd_norm_w=ssd_norm_w,
                   attn_out_norm_w=attn_out_norm_w, w_out=w_out, pre_mix_norm_w=pre_mix_norm_w,
                   post_mix_norm_w=post_mix_norm_w, pre_ffn_norm_w=pre_ffn_norm_w,
                   post_ffn_norm_w=post_ffn_norm_w, w_gate=w_gate, w_up=w_up, w_down=w_down)
    b, s, _ = x.shape
    pos_col = positions.astype(F32).reshape(b * s, 1)
    h = x
    for l in range(w_in.shape[0]):
        h = _layer(h, pos_col, {k: v[l] for k, v in stacked.items()})
    return h
```

```python
import functools

import numpy as np
import jax
import jax.numpy as jnp
from jax import lax
from jax.experimental import pallas as pl
from jax.experimental.pallas import tpu as pltpu

F32 = jnp.float32
BF16 = jnp.bfloat16

V_HEAD_DIM = 128
QK_NOPE_DIM = 128
QK_ROPE_DIM = 64
QK_HEAD_DIM = QK_NOPE_DIM + QK_ROPE_DIM
Q_LORA_RANK = 512
KV_LORA_RANK = 512
ROPE_THETA = 10000.0
SSD_HEAD_DIM = 64
SSD_GROUPS = 2
SSD_STATE = 128
SSD_CONV = 4
SSD_CHUNK = 128
EPS = 1e-6

LANES = 128
SUBLANES = 8
VMEM_LIMIT_BYTES = 56 * 1024 * 1024

QK_PAD = 2 * LANES
ROPE_HALF = QK_ROPE_DIM // 2


def _rms(t, w):
    return t * lax.rsqrt(jnp.mean(t * t, axis=-1, keepdims=True) + EPS) * w


def _sigmoid(t):
    return 1.0 / (1.0 + jnp.exp(-t))


def _rope(t, cos, sin_signed):
    return t * cos + pltpu.roll(t, LANES // 2, axis=1) * sin_signed


def _in_proj_kernel(x_ref, pos_ref, wpre_ref, win_ref, qnw_ref, wuq_ref, kvnw_ref, wukv_ref,
                    freq_ref, sgn_ref, q_ref, kv_ref, kr_ref, z_ref, xbc_ref, dt_ref,
                    *, n_heads, d_z, d_xbc, q_scale):
    u = _rms(x_ref[...], wpre_ref[...]).astype(BF16)
    ang = pos_ref[...] * freq_ref[...]
    cos = jnp.cos(ang)
    sin = jnp.sin(ang) * sgn_ref[...]

    o0 = 0
    o1 = o0 + Q_LORA_RANK + KV_LORA_RANK
    o2 = o1 + d_z
    o3 = o2 + d_xbc
    cqkv = jnp.dot(u, win_ref[:, o0:o1], preferred_element_type=F32)
    cqn = _rms(cqkv[:, :Q_LORA_RANK], qnw_ref[...]).astype(BF16)
    ckvn = _rms(cqkv[:, Q_LORA_RANK:], kvnw_ref[...]).astype(BF16)

    q = jnp.dot(cqn, wuq_ref[...], preferred_element_type=F32)
    cos_q = cos * q_scale
    sin_q = sin * q_scale
    for h in range(n_heads):
        base = h * QK_PAD
        q_ref[:, base:base + LANES] = (q[:, base:base + LANES] * q_scale).astype(BF16)
        q_ref[:, base + LANES:base + QK_PAD] = _rope(q[:, base + LANES:base + QK_PAD], cos_q, sin_q).astype(BF16)

    kv_ref[...] = jnp.dot(ckvn, wukv_ref[...], preferred_element_type=F32).astype(BF16)
    z_ref[...] = jnp.dot(u, win_ref[:, o1:o2], preferred_element_type=F32).astype(BF16)
    xbc_ref[...] = jnp.dot(u, win_ref[:, o2:o3], preferred_element_type=F32).astype(BF16)
    misc = jnp.dot(u, win_ref[:, o3:o3 + 2 * LANES], preferred_element_type=F32)
    kr_ref[...] = _rope(misc[:, :LANES], cos, sin).astype(BF16)
    dt_ref[...] = misc[:, LANES:]


def _mla_kernel(q_ref, kv_ref, kr_ref, o_ref, kc_ref, vt_ref, sa_ref, sb_ref, m_ref, l_ref, acc_ref, *, tile):
    qi = pl.program_id(2)
    n_tiles = vt_ref.shape[0]

    @pl.when(qi == 0)
    def _():
        kc_ref[:, :LANES] = kv_ref[0, :, :LANES]
        kc_ref[:, LANES:] = kr_ref[0]
        for c in range(n_tiles):
            vt_ref[c] = kv_ref[0, c * tile:(c + 1) * tile, LANES:].T

    q_t = q_ref[0].T
    m_ref[...] = jnp.full_like(m_ref, -jnp.inf)
    l_ref[...] = jnp.zeros_like(l_ref)
    acc_ref[...] = jnp.zeros_like(acc_ref)

    def scores(j, s_ref):
        rows = pl.ds(pl.multiple_of(j * tile, tile), tile)
        s_ref[...] = jnp.dot(kc_ref[rows, :], q_t, preferred_element_type=F32)

    def consume(j, s_ref, masked):
        s = s_ref[...]
        if masked:
            k_idx = lax.broadcasted_iota(jnp.int32, s.shape, 0)
            q_idx = lax.broadcasted_iota(jnp.int32, s.shape, 1)
            s = jnp.where(k_idx <= q_idx, s, -jnp.inf)
        m_prev = m_ref[...]
        m_new = jnp.maximum(m_prev, jnp.max(s, axis=0, keepdims=True))
        alpha = jnp.exp(m_prev - m_new)
        p = jnp.exp(s - m_new)
        l_ref[...] = alpha * l_ref[...] + jnp.sum(p, axis=0, keepdims=True)
        acc_ref[...] = alpha * acc_ref[...] + jnp.dot(vt_ref[j], p.astype(BF16), preferred_element_type=F32)
        m_ref[...] = m_new

    def pair(jj, carry):
        j = 2 * jj
        scores(j + 1, sb_ref)
        consume(j, sa_ref, False)
        scores(j + 2, sa_ref)
        consume(j + 1, sb_ref, False)
        return carry

    scores(0, sa_ref)
    lax.fori_loop(0, qi // 2, pair, 0)

    @pl.when(qi % 2 == 0)
    def _():
        consume(qi, sa_ref, True)

    @pl.when(qi % 2 == 1)
    def _():
        scores(qi, sb_ref)
        consume(qi - 1, sa_ref, False)
        consume(qi, sb_ref, True)

    o_ref[0] = (acc_ref[...] / l_ref[...]).T.astype(o_ref.dtype)


def _split2(t):
    hi = t.astype(BF16)
    return hi, (t - hi.astype(F32)).astype(BF16)


def _ssd_kernel(xbc_ref, z_ref, dt_ref, convw_ref, convb_ref, dtb_ref, aneg_ref, dskip_ref, nw_ref,
                tri3_ref, e2_ref, o_ref, ext_ref, state_ref, *, d_inner, n_heads):
    T, N, P, G = SSD_CHUNK, SSD_STATE, SSD_HEAD_DIM, SSD_GROUPS
    gw = d_inner // G
    halo = SUBLANES

    @pl.when(pl.program_id(1) == 0)
    def _():
        ext_ref[0:halo, :] = jnp.zeros((halo, ext_ref.shape[1]), F32)
        state_ref[...] = jnp.zeros_like(state_ref)

    ext_ref[halo:halo + T, :] = xbc_ref[0].astype(F32)
    conv = convb_ref[...] + convw_ref[0:1, :] * ext_ref[halo - 3:halo - 3 + T, :]
    for k in range(1, SSD_CONV):
        conv = conv + convw_ref[k:k + 1, :] * ext_ref[halo - 3 + k:halo - 3 + k + T, :]
    ext_ref[0:halo, :] = ext_ref[T:T + halo, :]
    xbc = conv * _sigmoid(conv)
    xs = xbc[:, :d_inner]
    bm = xbc[:, d_inner:d_inner + G * N]
    cm = xbc[:, d_inner + G * N:]

    dt_in = dt_ref[0] + dtb_ref[...]
    dt = jnp.maximum(dt_in, 0.0) + jnp.log1p(jnp.exp(-jnp.abs(dt_in)))
    a = dt * aneg_ref[...]
    a_hi = a.astype(BF16)
    a_r = a - a_hi.astype(F32)
    a_mid = a_r.astype(BF16)
    a_lo = (a_r - a_mid.astype(F32)).astype(BF16)
    a_cum = jnp.dot(tri3_ref[...], jnp.concatenate([a_hi, a_mid, a_lo], axis=0), preferred_element_type=F32)
    a_cum_t = a_cum.T
    a_last = a_cum[T - 1:T, :]
    ea = jnp.exp(a_cum)
    ds = jnp.exp(a_last - a_cum)

    v_hi, v_lo = _split2(jnp.concatenate([dt, ea, ds], axis=0))
    v_exp = jnp.dot(jnp.concatenate([v_hi, v_lo], axis=1), e2_ref[...], preferred_element_type=F32)
    dt_x, ea_x, ds_x = v_exp[0:T], v_exp[T:2 * T], v_exp[2 * T:3 * T]

    xdt = xs * dt_x
    xdt_b = xdt.astype(BF16)
    xw_b = (xdt * ds_x).astype(BF16)

    row = lax.broadcasted_iota(jnp.int32, (T, T), 0)
    col = lax.broadcasted_iota(jnp.int32, (T, T), 1)
    tri = col <= row
    lane = lax.broadcasted_iota(jnp.int32, (T, 2 * P), 1)
    heads_per_group = n_heads // G

    y_parts = []
    for g in range(G):
        bm_g = bm[:, g * N:(g + 1) * N]
        cm_g = cm[:, g * N:(g + 1) * N].astype(BF16)
        cb = lax.dot_general(cm_g, bm_g.astype(BF16), (((1,), (1,)), ((), ())), preferred_element_type=F32)
        for j in range(heads_per_group // 2):
            w_pair = []
            for h in (g * heads_per_group + 2 * j, g * heads_per_group + 2 * j + 1):
                seg = a_cum[:, h:h + 1] - a_cum_t[h:h + 1, :]
                w_pair.append((cb * jnp.exp(jnp.where(tri, seg, -jnp.inf))).astype(BF16))
            c0 = g * gw + 2 * j * P
            xp = xdt_b[:, c0:c0 + 2 * P]
            zero = jnp.zeros_like(xp)
            x_pair = jnp.concatenate([jnp.where(lane < P, xp, zero), jnp.where(lane >= P, xp, zero)], axis=0)
            y_parts.append(jnp.dot(jnp.concatenate(w_pair, axis=1), x_pair, preferred_element_type=F32))
    y_diag = jnp.concatenate(y_parts, axis=1)

    y_off_parts = []
    for g in range(G):
        cols = slice(g * gw, (g + 1) * gw)
        bm_t = bm[:, g * N:(g + 1) * N].T.astype(BF16)
        cm_g = cm[:, g * N:(g + 1) * N].astype(BF16)
        prev = state_ref[g]
        y_off_parts.append(jnp.dot(cm_g, prev.astype(BF16), preferred_element_type=F32))
        new = jnp.dot(bm_t, xw_b[:, cols], preferred_element_type=F32)
        state_ref[g] = prev * ea_x[T - 1:T, cols] + new
    y_off = jnp.concatenate(y_off_parts, axis=1) * ea_x

    y = y_diag + y_off + xs * dskip_ref[...]
    zf = z_ref[0].astype(F32)
    gated = y * (zf * _sigmoid(zf))
    outs = []
    for g in range(G):
        gg = gated[:, g * gw:(g + 1) * gw]
        outs.append(gg * lax.rsqrt(jnp.mean(gg * gg, axis=-1, keepdims=True) + EPS))
    o_ref[0] = (jnp.concatenate(outs, axis=1) * nw_ref[...]).astype(o_ref.dtype)


def _out_proj_kernel(x_ref, attn_ref, ssm_ref, anw_ref, woa_ref, wos_ref, postw_ref, h_ref):
    attn_n = _rms(attn_ref[...].astype(F32), anw_ref[...]).astype(BF16)
    mix = jnp.dot(attn_n, woa_ref[...], preferred_element_type=F32)
    mix = mix + jnp.dot(ssm_ref[...], wos_ref[...], preferred_element_type=F32)
    h_ref[...] = x_ref[...] + _rms(mix, postw_ref[...])


def _ffn_kernel(h_ref, prew_ref, wg_ref, wu_ref, wd_ref, postw_ref, o_ref, v_ref):
    f = pl.program_id(1)

    @pl.when(f == 0)
    def _():
        v_ref[...] = _rms(h_ref[...], prew_ref[...]).astype(BF16)

    v = v_ref[...]
    gate = jnp.dot(v, wg_ref[...], preferred_element_type=F32)
    up = jnp.dot(v, wu_ref[...], preferred_element_type=F32)
    act = (gate * _sigmoid(gate) * up).astype(BF16)
    part = jnp.dot(act, wd_ref[...], preferred_element_type=F32)

    @pl.when(f == 0)
    def _():
        o_ref[...] = part

    @pl.when(f > 0)
    def _():
        o_ref[...] += part

    @pl.when(f == pl.num_programs(1) - 1)
    def _():
        o_ref[...] = h_ref[...] + _rms(o_ref[...], postw_ref[...])


def _tile(n, want):
    t = min(n, want)
    assert n % t == 0, (n, t)
    return t


def _resident(shape):
    return pl.BlockSpec(shape, lambda *_: (0,) * len(shape), pipeline_mode=pl.Buffered(1))


def _row(w):
    return w.reshape(1, -1).astype(F32)


def _layer(h, pos_col, p):
    b, s, d_model = h.shape
    m = b * s
    n_mla_heads = p["w_uq"].shape[1] // QK_HEAD_DIM
    mla_width = n_mla_heads * V_HEAD_DIM
    d_inner = p["ssd_norm_w"].shape[0]
    n_ssd_heads = p["dt_bias"].shape[0]
    d_xbc = d_inner + 2 * SSD_GROUPS * SSD_STATE
    d_ff = p["w_gate"].shape[1]
    assert n_ssd_heads * SSD_HEAD_DIM == d_inner and n_ssd_heads <= LANES
    assert s % SSD_CHUNK == 0

    w_in = p["w_in"]
    offs = np.cumsum([0, Q_LORA_RANK, KV_LORA_RANK, QK_ROPE_DIM, d_inner, d_xbc, n_ssd_heads])
    w_cq, w_ckv, w_kr, w_z, w_xbc, w_dt = [w_in[:, offs[i]:offs[i + 1]] for i in range(6)]
    zpad = lambda n: jnp.zeros((d_model, n), w_in.dtype)
    w_in_r = jnp.concatenate([
        w_cq, w_ckv, w_z, w_xbc,
        w_kr[:, :ROPE_HALF], zpad(ROPE_HALF), w_kr[:, ROPE_HALF:], zpad(ROPE_HALF),
        w_dt, zpad(LANES - n_ssd_heads)], axis=1).astype(BF16)
    d_in_r = w_in_r.shape[1]

    wq3 = p["w_uq"].reshape(Q_LORA_RANK, n_mla_heads, QK_HEAD_DIM)
    zq = jnp.zeros((Q_LORA_RANK, n_mla_heads, ROPE_HALF), wq3.dtype)
    w_uq_r = jnp.concatenate([
        wq3[..., :QK_NOPE_DIM], wq3[..., QK_NOPE_DIM:QK_NOPE_DIM + ROPE_HALF], zq,
        wq3[..., QK_NOPE_DIM + ROPE_HALF:], zq], axis=-1).reshape(Q_LORA_RANK, n_mla_heads * QK_PAD).astype(BF16)
    w_ukv = p["w_ukv"].astype(BF16)

    inv_freq = ROPE_THETA ** (-jnp.arange(0, QK_ROPE_DIM, 2, dtype=F32) / QK_ROPE_DIM)
    z32 = jnp.zeros((ROPE_HALF,), F32)
    o32 = jnp.ones((ROPE_HALF,), F32)
    freq_tab = jnp.concatenate([inv_freq, z32, inv_freq, z32]).reshape(1, LANES)
    sgn_tab = jnp.concatenate([-o32, z32, o32, z32]).reshape(1, LANES)

    tm = _tile(m, 256)
    row_spec = lambda n: pl.BlockSpec((tm, n), lambda i: (i, 0))
    q, kv, kr, z, xbc, dt_raw = pl.pallas_call(
        functools.partial(_in_proj_kernel, n_heads=n_mla_heads, d_z=d_inner, d_xbc=d_xbc,
                          q_scale=float(QK_HEAD_DIM) ** -0.5),
        grid=(m // tm,),
        in_specs=[row_spec(d_model), row_spec(1), _resident((1, d_model)), _resident((d_model, d_in_r)),
                  _resident((1, Q_LORA_RANK)), _resident((Q_LORA_RANK, n_mla_heads * QK_PAD)),
                  _resident((1, KV_LORA_RANK)), _resident((KV_LORA_RANK, w_ukv.shape[1])),
                  _resident((1, LANES)), _resident((1, LANES))],
        out_specs=[row_spec(n_mla_heads * QK_PAD), row_spec(w_ukv.shape[1]), row_spec(LANES),
                   row_spec(d_inner), row_spec(d_xbc), row_spec(LANES)],
        out_shape=[jax.ShapeDtypeStruct((m, n_mla_heads * QK_PAD), BF16),
                   jax.ShapeDtypeStruct((m, w_ukv.shape[1]), BF16),
                   jax.ShapeDtypeStruct((m, LANES), BF16),
                   jax.ShapeDtypeStruct((m, d_inner), BF16),
                   jax.ShapeDtypeStruct((m, d_xbc), BF16),
                   jax.ShapeDtypeStruct((m, LANES), F32)],
        compiler_params=pltpu.CompilerParams(dimension_semantics=("arbitrary",),
                                             vmem_limit_bytes=VMEM_LIMIT_BYTES),
        name="in_proj",
    )(h.reshape(m, d_model), pos_col, _row(p["pre_mix_norm_w"]), w_in_r, _row(p["q_norm_w"]), w_uq_r,
      _row(p["kv_norm_w"]), w_ukv, freq_tab, sgn_tab)

    ta = _tile(s, 512)
    attn = pl.pallas_call(
        functools.partial(_mla_kernel, tile=ta),
        grid=(b, n_mla_heads, s // ta),
        in_specs=[pl.BlockSpec((1, ta, QK_PAD), lambda bi, hi, qi: (bi, qi, hi)),
                  pl.BlockSpec((1, s, 2 * LANES), lambda bi, hi, qi: (bi, 0, hi)),
                  pl.BlockSpec((1, s, LANES), lambda bi, hi, qi: (bi, 0, 0))],
        out_specs=pl.BlockSpec((1, ta, V_HEAD_DIM), lambda bi, hi, qi: (bi, qi, hi)),
        out_shape=jax.ShapeDtypeStruct((b, s, mla_width), BF16),
        scratch_shapes=[pltpu.VMEM((s, QK_PAD), BF16), pltpu.VMEM((s // ta, V_HEAD_DIM, ta), BF16),
                        pltpu.VMEM((ta, ta), F32), pltpu.VMEM((ta, ta), F32), pltpu.VMEM((1, ta), F32), pltpu.VMEM((1, ta), F32), pltpu.VMEM((V_HEAD_DIM, ta), F32)],
        compiler_params=pltpu.CompilerParams(dimension_semantics=("arbitrary", "arbitrary", "arbitrary"),
                                             vmem_limit_bytes=VMEM_LIMIT_BYTES),
        name="mla",
    )(q.reshape(b, s, -1), kv.reshape(b, s, -1), kr.reshape(b, s, LANES))

    T = SSD_CHUNK
    lane_pad = lambda v: jnp.pad(v.astype(F32), (0, LANES - n_ssd_heads)).reshape(1, LANES)
    a_neg = lane_pad(-jnp.exp(p["a_log"].astype(F32)))
    dt_bias = lane_pad(p["dt_bias"])
    d_skip_x = jnp.repeat(p["d_skip"].astype(F32), SSD_HEAD_DIM).reshape(1, d_inner)
    tri = jnp.tril(jnp.ones((T, T), BF16))
    tri3 = jnp.concatenate([tri, tri, tri], axis=1)
    expand = (jnp.arange(LANES)[:, None] == (jnp.arange(d_inner)[None, :] // SSD_HEAD_DIM)).astype(BF16)
    expand2 = jnp.concatenate([expand, expand], axis=0)
    chunk_spec = lambda n: pl.BlockSpec((1, T, n), lambda bi, ci: (bi, ci, 0))
    ssm = pl.pallas_call(
        functools.partial(_ssd_kernel, d_inner=d_inner, n_heads=n_ssd_heads),
        grid=(b, s // T),
        in_specs=[chunk_spec(d_xbc), chunk_spec(d_inner), chunk_spec(LANES),
                  _resident((SSD_CONV, d_xbc)), _resident((1, d_xbc)), _resident((1, LANES)),
                  _resident((1, LANES)), _resident((1, d_inner)), _resident((1, d_inner)),
                  _resident((T, 3 * T)), _resident((2 * LANES, d_inner))],
        out_specs=chunk_spec(d_inner),
        out_shape=jax.ShapeDtypeStruct((b, s, d_inner), BF16),
        scratch_shapes=[pltpu.VMEM((T + 2 * SUBLANES, d_xbc), F32),
                        pltpu.VMEM((SSD_GROUPS, SSD_STATE, d_inner // SSD_GROUPS), F32)],
        compiler_params=pltpu.CompilerParams(dimension_semantics=("arbitrary", "arbitrary"),
                                             vmem_limit_bytes=VMEM_LIMIT_BYTES),
        name="ssd",
    )(xbc.reshape(b, s, d_xbc), z.reshape(b, s, d_inner), dt_raw.reshape(b, s, LANES),
      p["conv_w"].astype(F32), _row(p["conv_b"]), dt_bias, a_neg, d_skip_x, _row(p["ssd_norm_w"]),
      tri3, expand2)

    w_out = p["w_out"].astype(BF16)
    to = _tile(m, 512)
    orow = lambda n: pl.BlockSpec((to, n), lambda i: (i, 0))
    h1 = pl.pallas_call(
        _out_proj_kernel,
        grid=(m // to,),
        in_specs=[orow(d_model), orow(mla_width), orow(d_inner), _resident((1, mla_width)),
                  _resident((mla_width, d_model)), _resident((d_inner, d_model)), _resident((1, d_model))],
        out_specs=orow(d_model),
        out_shape=jax.ShapeDtypeStruct((m, d_model), F32),
        compiler_params=pltpu.CompilerParams(dimension_semantics=("arbitrary",),
                                             vmem_limit_bytes=VMEM_LIMIT_BYTES),
        name="out_proj",
    )(h.reshape(m, d_model), attn.reshape(m, mla_width), ssm.reshape(m, d_inner), _row(p["attn_out_norm_w"]),
      w_out[:mla_width], w_out[mla_width:], _row(p["post_mix_norm_w"]))

    tf_m = _tile(m, 512)
    tf_f = _tile(d_ff, 512)
    out = pl.pallas_call(
        _ffn_kernel,
        grid=(m // tf_m, d_ff // tf_f),
        in_specs=[pl.BlockSpec((tf_m, d_model), lambda i, f: (i, 0)),
                  _resident((1, d_model)),
                  pl.BlockSpec((d_model, tf_f), lambda i, f: (0, f)),
                  pl.BlockSpec((d_model, tf_f), lambda i, f: (0, f)),
                  pl.BlockSpec((tf_f, d_model), lambda i, f: (f, 0)),
                  _resident((1, d_model))],
        out_specs=pl.BlockSpec((tf_m, d_model), lambda i, f: (i, 0)),
        out_shape=jax.ShapeDtypeStruct((m, d_model), F32),
        scratch_shapes=[pltpu.VMEM((tf_m, d_model), BF16)],
        compiler_params=pltpu.CompilerParams(dimension_semantics=("arbitrary", "arbitrary"),
                                             vmem_limit_bytes=VMEM_LIMIT_BYTES),
        name="ffn",
    )(h1, _row(p["pre_ffn_norm_w"]), p["w_gate"].astype(BF16), p["w_up"].astype(BF16),
      p["w_down"].astype(BF16), _row(p["post_ffn_norm_w"]))
    return out.reshape(b, s, d_model)


def kernel(x, positions, w_in, q_norm_w, w_uq, kv_norm_w, w_ukv, conv_w, conv_b, dt_bias, a_log, d_skip,
           ssd_norm_w, attn_out_norm_w, w_out, pre_mix_norm_w, post_mix_norm_w, pre_ffn_norm_w,
           post_ffn_norm_w, w_gate, w_up, w_down):
    stacked = dict(w_in=w_in, q_norm_w=q_norm_w, w_uq=w_uq, kv_norm_w=kv_norm_w, w_ukv=w_ukv, conv_w=conv_w,
                   conv_b=conv_b, dt_bias=dt_bias, a_log=a_log, d_skip=d_skip, ssd_norm_w=ssd_norm_w,
                   attn_out_norm_w=attn_out_norm_w, w_out=w_out, pre_mix_norm_w=pre_mix_norm_w,
                   post_mix_norm_w=post_mix_norm_w, pre_ffn_norm_w=pre_ffn_norm_w,
                   post_ffn_norm_w=post_ffn_norm_w, w_gate=w_gate, w_up=w_up, w_down=w_down)
    b, s, _ = x.shape
    pos_col = positions.astype(F32).reshape(b * s, 1)
    h = x
    for l in range(w_in.shape[0]):
        h = _layer(h, pos_col, {k: v[l] for k, v in stacked.items()})
    return h
```

```python
import functools

import numpy as np
import jax
import jax.numpy as jnp
from jax import lax
from jax.experimental import pallas as pl
from jax.experimental.pallas import tpu as pltpu

F32 = jnp.float32
BF16 = jnp.bfloat16

V_HEAD_DIM = 128
QK_NOPE_DIM = 128
QK_ROPE_DIM = 64
QK_HEAD_DIM = QK_NOPE_DIM + QK_ROPE_DIM
Q_LORA_RANK = 512
KV_LORA_RANK = 512
ROPE_THETA = 10000.0
SSD_HEAD_DIM = 64
SSD_GROUPS = 2
SSD_STATE = 128
SSD_CONV = 4
SSD_CHUNK = 128
EPS = 1e-6

LANES = 128
SUBLANES = 8
VMEM_LIMIT_BYTES = 56 * 1024 * 1024

QK_PAD = 2 * LANES
ROPE_HALF = QK_ROPE_DIM // 2
LOG2_E = 1.4426950408889634
ONES_ROWS = 2 * SUBLANES


def _rms(t, w):
    return t * lax.rsqrt(jnp.mean(t * t, axis=-1, keepdims=True) + EPS) * w


def _silu(t):
    h = 0.5 * t
    return h + h * jnp.tanh(h)


def _rope(t, cos, sin_signed):
    return t * cos + pltpu.roll(t, LANES // 2, axis=1) * sin_signed


def _in_proj_kernel(x_ref, pos_ref, wpre_ref, win_ref, qnw_ref, wuq_ref, kvnw_ref, wukv_ref,
                    freq_ref, sgn_ref, q_ref, kv_ref, kr_ref, z_ref, xbc_ref, dt_ref,
                    *, n_heads, d_z, d_xbc, q_scale):
    u = _rms(x_ref[...], wpre_ref[...]).astype(BF16)
    ang = pos_ref[...] * freq_ref[...]
    cos = jnp.cos(ang)
    sin = jnp.sin(ang) * sgn_ref[...]

    o0 = 0
    o1 = o0 + Q_LORA_RANK + KV_LORA_RANK
    o2 = o1 + d_z
    o3 = o2 + d_xbc
    cqkv = jnp.dot(u, win_ref[:, o0:o1], preferred_element_type=F32)
    cqn = _rms(cqkv[:, :Q_LORA_RANK], qnw_ref[...]).astype(BF16)
    ckvn = _rms(cqkv[:, Q_LORA_RANK:], kvnw_ref[...]).astype(BF16)

    q = jnp.dot(cqn, wuq_ref[...], preferred_element_type=F32)
    cos_q = cos * q_scale
    sin_q = sin * q_scale
    for h in range(n_heads):
        base = h * QK_PAD
        q_ref[:, base:base + LANES] = (q[:, base:base + LANES] * q_scale).astype(BF16)
        q_ref[:, base + LANES:base + QK_PAD] = _rope(q[:, base + LANES:base + QK_PAD], cos_q, sin_q).astype(BF16)

    kv_ref[...] = jnp.dot(ckvn, wukv_ref[...], preferred_element_type=F32).astype(BF16)
    z_ref[...] = jnp.dot(u, win_ref[:, o1:o2], preferred_element_type=F32).astype(BF16)
    xbc_ref[...] = jnp.dot(u, win_ref[:, o2:o3], preferred_element_type=F32).astype(BF16)
    misc = jnp.dot(u, win_ref[:, o3:o3 + 2 * LANES], preferred_element_type=F32)
    kr_ref[...] = _rope(misc[:, :LANES], cos, sin).astype(BF16)
    dt_ref[...] = misc[:, LANES:]


def _mla_kernel(q_ref, kv_ref, kr_ref, o_ref, kc_ref, vt_ref, sa_ref, sb_ref, m_ref, acc_ref, *, tile, heads):
    qi = pl.program_id(2)
    n_tiles = vt_ref.shape[1]
    kv_w = 2 * LANES

    @pl.when(qi == 0)
    def _():
        for h in range(heads):
            kc_ref[h, :, :LANES] = kv_ref[0, :, h * kv_w:h * kv_w + LANES]
            kc_ref[h, :, LANES:] = kr_ref[0]
            for c in range(n_tiles):
                vt_ref[h, c, :V_HEAD_DIM] = kv_ref[0, c * tile:(c + 1) * tile, h * kv_w + LANES:(h + 1) * kv_w].T
                vt_ref[h, c, V_HEAD_DIM:] = jnp.ones((ONES_ROWS, tile), BF16)

    q_t = [q_ref[0, :, h * QK_PAD:(h + 1) * QK_PAD].T for h in range(heads)]
    m_ref[...] = jnp.full_like(m_ref, -jnp.inf)
    acc_ref[...] = jnp.zeros_like(acc_ref)

    def scores(j, s_ref):
        rows = pl.ds(pl.multiple_of(j * tile, tile), tile)
        for h in range(heads):
            s_ref[h] = jnp.dot(kc_ref[h, rows, :], q_t[h], preferred_element_type=F32)

    def consume(j, s_ref, masked):
        for h in range(heads):
            s = s_ref[h]
            if masked:
                k_idx = lax.broadcasted_iota(jnp.int32, s.shape, 0)
                q_idx = lax.broadcasted_iota(jnp.int32, s.shape, 1)
                s = jnp.where(k_idx <= q_idx, s, -jnp.inf)
            m_prev = m_ref[h]
            m_new = jnp.maximum(m_prev, jnp.max(s, axis=0, keepdims=True))
            alpha = jnp.exp2(m_prev - m_new)
            p = jnp.exp2(s - m_new)
            acc_ref[h] = alpha * acc_ref[h] + jnp.dot(vt_ref[h, j], p.astype(BF16), preferred_element_type=F32)
            m_ref[h] = m_new

    def pair(jj, carry):
        j = 2 * jj
        scores(j + 1, sb_ref)
        consume(j, sa_ref, False)
        scores(j + 2, sa_ref)
        consume(j + 1, sb_ref, False)
        return carry

    scores(0, sa_ref)
    lax.fori_loop(0, qi // 2, pair, 0)

    @pl.when(qi % 2 == 0)
    def _():
        consume(qi, sa_ref, True)

    @pl.when(qi % 2 == 1)
    def _():
        scores(qi, sb_ref)
        consume(qi - 1, sa_ref, False)
        consume(qi, sb_ref, True)

    for h in range(heads):
        out = acc_ref[h, :V_HEAD_DIM] / acc_ref[h, V_HEAD_DIM:V_HEAD_DIM + 1]
        o_ref[0, :, h * V_HEAD_DIM:(h + 1) * V_HEAD_DIM] = out.T.astype(o_ref.dtype)


def _ssd_kernel(xbc_ref, z_ref, dt_ref, shift_ref, convw_ref, convb_ref, dtb_ref, aneg_ref, dskip_ref, nw_ref,
                tri_ref, e_ref, o_ref, xx_ref, hilo_ref, state_ref, *, d_inner, n_heads):
    T, N, P, G = SSD_CHUNK, SSD_STATE, SSD_HEAD_DIM, SSD_GROUPS
    gw = d_inner // G

    chunk = pl.program_id(1)
    parity = chunk % 2

    @pl.when(chunk == 0)
    def _():
        xx_ref[0:T, :] = jnp.zeros((T, xx_ref.shape[1]), BF16)
        state_ref[...] = jnp.zeros_like(state_ref)

    xx_ref[pl.ds(pl.multiple_of((1 - parity) * T, T), T), :] = xbc_ref[0]
    taps = jnp.dot(shift_ref[parity], xx_ref[...], preferred_element_type=F32)
    conv = convb_ref[...] + convw_ref[SSD_CONV - 1:SSD_CONV, :] * xbc_ref[0].astype(F32)
    for k in range(SSD_CONV - 1):
        conv = conv + convw_ref[k:k + 1, :] * taps[k * T:(k + 1) * T]
    xbc = _silu(conv)
    xs = xbc[:, :d_inner]
    bm = xbc[:, d_inner:d_inner + G * N]
    cm = xbc[:, d_inner + G * N:]

    dt_in = dt_ref[0] + dtb_ref[...]
    dt = jnp.maximum(dt_in, 0.0) + jnp.log1p(jnp.exp(-jnp.abs(dt_in)))
    a = dt * aneg_ref[...]
    a_hi = a.astype(BF16)
    a_r = a - a_hi.astype(F32)
    a_mid = a_r.astype(BF16)
    a_lo = (a_r - a_mid.astype(F32)).astype(BF16)
    tri_b = tri_ref[...]
    a_cum = (jnp.dot(tri_b, a_hi, preferred_element_type=F32) + jnp.dot(tri_b, a_mid, preferred_element_type=F32)
             + jnp.dot(tri_b, a_lo, preferred_element_type=F32))
    a_cum_t = a_cum.T
    a_last = a_cum[T - 1:T, :]
    ea = jnp.exp2(a_cum)
    ds = jnp.exp2(a_last - a_cum)

    stacked = jnp.concatenate([dt, ea, ds], axis=0)
    s_hi = stacked.astype(BF16)
    s_lo = (stacked - s_hi.astype(F32)).astype(BF16)
    hilo_ref[:, :LANES] = s_hi
    hilo_ref[:, LANES:] = s_lo
    expanded = jnp.dot(hilo_ref[...], e_ref[...], preferred_element_type=F32)
    dt_x, ea_x, ds_x = expanded[0:T], expanded[T:2 * T], expanded[2 * T:3 * T]

    xdt = xs * dt_x
    xdt_b = xdt.astype(BF16)
    xw_b = (xdt * ds_x).astype(BF16)

    row = lax.broadcasted_iota(jnp.int32, (T, T), 0)
    col = lax.broadcasted_iota(jnp.int32, (T, T), 1)
    tri = col <= row
    lane = lax.broadcasted_iota(jnp.int32, (T, 2 * P), 1)
    heads_per_group = n_heads // G

    y_parts = []
    y_off_parts = []
    for g in range(G):
        cols = slice(g * gw, (g + 1) * gw)
        bm_g = bm[:, g * N:(g + 1) * N]
        cm_g = cm[:, g * N:(g + 1) * N].astype(BF16)
        cb = lax.dot_general(cm_g, bm_g.astype(BF16), (((1,), (1,)), ((), ())), preferred_element_type=F32)
        for j in range(heads_per_group // 2):
            c0 = g * gw + 2 * j * P
            xp = xdt_b[:, c0:c0 + 2 * P]
            zero = jnp.zeros_like(xp)
            part = None
            for i, x_half in enumerate((jnp.where(lane < P, xp, zero), jnp.where(lane >= P, xp, zero))):
                h = g * heads_per_group + 2 * j + i
                seg = a_cum[:, h:h + 1] - a_cum_t[h:h + 1, :]
                w = (cb * jnp.exp2(jnp.where(tri, seg, -jnp.inf))).astype(BF16)
                d = jnp.dot(w, x_half, preferred_element_type=F32)
                part = d if part is None else part + d
            y_parts.append(part)
        prev = state_ref[g]
        y_off_parts.append(jnp.dot(cm_g, prev.astype(BF16), preferred_element_type=F32))
        new = jnp.dot(bm_g.T.astype(BF16), xw_b[:, cols], preferred_element_type=F32)
        state_ref[g] = prev * ea_x[T - 1:T, cols] + new
    y_diag = jnp.concatenate(y_parts, axis=1)
    y_off = jnp.concatenate(y_off_parts, axis=1) * ea_x

    y = y_diag + y_off + xs * dskip_ref[...]
    gated = y * _silu(z_ref[0].astype(F32))
    outs = []
    for g in range(G):
        gg = gated[:, g * gw:(g + 1) * gw]
        outs.append(gg * lax.rsqrt(jnp.mean(gg * gg, axis=-1, keepdims=True) + EPS))
    o_ref[0] = (jnp.concatenate(outs, axis=1) * nw_ref[...]).astype(o_ref.dtype)


def _out_proj_kernel(x_ref, attn_ref, ssm_ref, anw_ref, woa_ref, wos_ref, postw_ref, h_ref):
    attn_n = _rms(attn_ref[...].astype(F32), anw_ref[...]).astype(BF16)
    mix = jnp.dot(attn_n, woa_ref[...], preferred_element_type=F32)
    mix = mix + jnp.dot(ssm_ref[...], wos_ref[...], preferred_element_type=F32)
    h_ref[...] = x_ref[...] + _rms(mix, postw_ref[...])


def _ffn_kernel(h_ref, prew_ref, wg_ref, wu_ref, wd_ref, postw_ref, o_ref, v_ref):
    f = pl.program_id(1)

    @pl.when(f == 0)
    def _():
        v_ref[...] = _rms(h_ref[...], prew_ref[...]).astype(BF16)
        o_ref[...] = jnp.zeros_like(o_ref)

    v = v_ref[...]
    gate = jnp.dot(v, wg_ref[...], preferred_element_type=F32)
    up = jnp.dot(v, wu_ref[...], preferred_element_type=F32)
    act = (_silu(gate) * up).astype(BF16)
    o_ref[...] += jnp.dot(act, wd_ref[...], preferred_element_type=F32)

    @pl.when(f == pl.num_programs(1) - 1)
    def _():
        o_ref[...] = h_ref[...] + _rms(o_ref[...], postw_ref[...])


def _tile(n, want):
    t = min(n, want)
    assert n % t == 0, (n, t)
    return t


def _resident(shape):
    return pl.BlockSpec(shape, lambda *_: (0,) * len(shape), pipeline_mode=pl.Buffered(1))


def _row(w):
    return w.reshape(1, -1).astype(F32)


def _layer(h, pos_col, p):
    b, s, d_model = h.shape
    m = b * s
    n_mla_heads = p["w_uq"].shape[1] // QK_HEAD_DIM
    mla_width = n_mla_heads * V_HEAD_DIM
    d_inner = p["ssd_norm_w"].shape[0]
    n_ssd_heads = p["dt_bias"].shape[0]
    d_xbc = d_inner + 2 * SSD_GROUPS * SSD_STATE
    d_ff = p["w_gate"].shape[1]
    assert n_ssd_heads * SSD_HEAD_DIM == d_inner and n_ssd_heads <= LANES
    assert s % SSD_CHUNK == 0

    w_in = p["w_in"]
    offs = np.cumsum([0, Q_LORA_RANK, KV_LORA_RANK, QK_ROPE_DIM, d_inner, d_xbc, n_ssd_heads])
    w_cq, w_ckv, w_kr, w_z, w_xbc, w_dt = [w_in[:, offs[i]:offs[i + 1]] for i in range(6)]
    zpad = lambda n: jnp.zeros((d_model, n), w_in.dtype)
    w_in_r = jnp.concatenate([
        w_cq, w_ckv, w_z, w_xbc,
        w_kr[:, :ROPE_HALF], zpad(ROPE_HALF), w_kr[:, ROPE_HALF:], zpad(ROPE_HALF),
        w_dt, zpad(LANES - n_ssd_heads)], axis=1).astype(BF16)
    d_in_r = w_in_r.shape[1]

    wq3 = p["w_uq"].reshape(Q_LORA_RANK, n_mla_heads, QK_HEAD_DIM)
    zq = jnp.zeros((Q_LORA_RANK, n_mla_heads, ROPE_HALF), wq3.dtype)
    w_uq_r = jnp.concatenate([
        wq3[..., :QK_NOPE_DIM], wq3[..., QK_NOPE_DIM:QK_NOPE_DIM + ROPE_HALF], zq,
        wq3[..., QK_NOPE_DIM + ROPE_HALF:], zq], axis=-1).reshape(Q_LORA_RANK, n_mla_heads * QK_PAD).astype(BF16)
    w_ukv = p["w_ukv"].astype(BF16)

    inv_freq = ROPE_THETA ** (-jnp.arange(0, QK_ROPE_DIM, 2, dtype=F32) / QK_ROPE_DIM)
    z32 = jnp.zeros((ROPE_HALF,), F32)
    o32 = jnp.ones((ROPE_HALF,), F32)
    freq_tab = jnp.concatenate([inv_freq, z32, inv_freq, z32]).reshape(1, LANES)
    sgn_tab = jnp.concatenate([-o32, z32, o32, z32]).reshape(1, LANES)

    tm = _tile(m, 512)
    row_spec = lambda n: pl.BlockSpec((tm, n), lambda i: (i, 0))
    q, kv, kr, z, xbc, dt_raw = pl.pallas_call(
        functools.partial(_in_proj_kernel, n_heads=n_mla_heads, d_z=d_inner, d_xbc=d_xbc,
                          q_scale=float(QK_HEAD_DIM) ** -0.5 * LOG2_E),
        grid=(m // tm,),
        in_specs=[row_spec(d_model), row_spec(1), _resident((1, d_model)), _resident((d_model, d_in_r)),
                  _resident((1, Q_LORA_RANK)), _resident((Q_LORA_RANK, n_mla_heads * QK_PAD)),
                  _resident((1, KV_LORA_RANK)), _resident((KV_LORA_RANK, w_ukv.shape[1])),
                  _resident((1, LANES)), _resident((1, LANES))],
        out_specs=[row_spec(n_mla_heads * QK_PAD), row_spec(w_ukv.shape[1]), row_spec(LANES),
                   row_spec(d_inner), row_spec(d_xbc), row_spec(LANES)],
        out_shape=[jax.ShapeDtypeStruct((m, n_mla_heads * QK_PAD), BF16),
                   jax.ShapeDtypeStruct((m, w_ukv.shape[1]), BF16),
                   jax.ShapeDtypeStruct((m, LANES), BF16),
                   jax.ShapeDtypeStruct((m, d_inner), BF16),
                   jax.ShapeDtypeStruct((m, d_xbc), BF16),
                   jax.ShapeDtypeStruct((m, LANES), F32)],
        compiler_params=pltpu.CompilerParams(dimension_semantics=("arbitrary",),
                                             vmem_limit_bytes=VMEM_LIMIT_BYTES),
        name="in_proj",
    )(h.reshape(m, d_model), pos_col, _row(p["pre_mix_norm_w"]), w_in_r, _row(p["q_norm_w"]), w_uq_r,
      _row(p["kv_norm_w"]), w_ukv, freq_tab, sgn_tab)

    ta = _tile(s, 512)
    hp = 2 if n_mla_heads % 2 == 0 else 1
    attn = pl.pallas_call(
        functools.partial(_mla_kernel, tile=ta, heads=hp),
        grid=(b, n_mla_heads // hp, s // ta),
        in_specs=[pl.BlockSpec((1, ta, hp * QK_PAD), lambda bi, hi, qi: (bi, qi, hi)),
                  pl.BlockSpec((1, s, hp * 2 * LANES), lambda bi, hi, qi: (bi, 0, hi)),
                  pl.BlockSpec((1, s, LANES), lambda bi, hi, qi: (bi, 0, 0))],
        out_specs=pl.BlockSpec((1, ta, hp * V_HEAD_DIM), lambda bi, hi, qi: (bi, qi, hi)),
        out_shape=jax.ShapeDtypeStruct((b, s, mla_width), BF16),
        scratch_shapes=[pltpu.VMEM((hp, s, QK_PAD), BF16),
                        pltpu.VMEM((hp, s // ta, V_HEAD_DIM + ONES_ROWS, ta), BF16),
                        pltpu.VMEM((hp, ta, ta), F32), pltpu.VMEM((hp, ta, ta), F32), pltpu.VMEM((hp, 1, ta), F32),
                        pltpu.VMEM((hp, V_HEAD_DIM + ONES_ROWS, ta), F32)],
        compiler_params=pltpu.CompilerParams(dimension_semantics=("arbitrary", "arbitrary", "arbitrary"),
                                             vmem_limit_bytes=VMEM_LIMIT_BYTES),
        name="mla",
    )(q.reshape(b, s, -1), kv.reshape(b, s, -1), kr.reshape(b, s, LANES))

    T = SSD_CHUNK
    lane_pad = lambda v: jnp.pad(v.astype(F32), (0, LANES - n_ssd_heads)).reshape(1, LANES)
    a_neg = lane_pad(-jnp.exp(p["a_log"].astype(F32)) * LOG2_E)
    dt_bias = lane_pad(p["dt_bias"])
    d_skip_x = jnp.repeat(p["d_skip"].astype(F32), SSD_HEAD_DIM).reshape(1, d_inner)
    tri = jnp.tril(jnp.ones((T, T), BF16))
    tap_row = jnp.arange((SSD_CONV - 1) * T)
    src_row = tap_row % T - (SSD_CONV - 1 - tap_row // T)
    shift = jnp.stack([(jnp.arange(2 * T)[None, :] == ((cur0 + src_row) % (2 * T))[:, None]).astype(BF16)
                       for cur0 in (T, 0)])
    expand = (jnp.arange(LANES)[:, None] == (jnp.arange(d_inner)[None, :] // SSD_HEAD_DIM)).astype(BF16)
    chunk_spec = lambda n: pl.BlockSpec((1, T, n), lambda bi, ci: (bi, ci, 0))
    ssm = pl.pallas_call(
        functools.partial(_ssd_kernel, d_inner=d_inner, n_heads=n_ssd_heads),
        grid=(b, s // T),
        in_specs=[chunk_spec(d_xbc), chunk_spec(d_inner), chunk_spec(LANES), _resident((2, (SSD_CONV - 1) * T, 2 * T)),
                  _resident((SSD_CONV, d_xbc)), _resident((1, d_xbc)), _resident((1, LANES)),
                  _resident((1, LANES)), _resident((1, d_inner)), _resident((1, d_inner)),
                  _resident((T, T)), _resident((2 * LANES, d_inner))],
        out_specs=chunk_spec(d_inner),
        out_shape=jax.ShapeDtypeStruct((b, s, d_inner), BF16),
        scratch_shapes=[pltpu.VMEM((2 * T, d_xbc), BF16), pltpu.VMEM((3 * T, 2 * LANES), BF16),
                        pltpu.VMEM((SSD_GROUPS, SSD_STATE, d_inner // SSD_GROUPS), F32)],
        compiler_params=pltpu.CompilerParams(dimension_semantics=("arbitrary", "arbitrary"),
                                             vmem_limit_bytes=VMEM_LIMIT_BYTES),
        name="ssd",
    )(xbc.reshape(b, s, d_xbc), z.reshape(b, s, d_inner), dt_raw.reshape(b, s, LANES), shift,
      p["conv_w"].astype(F32), _row(p["conv_b"]), dt_bias, a_neg, d_skip_x, _row(p["ssd_norm_w"]),
      tri, jnp.concatenate([expand, expand], axis=0))

    w_out = p["w_out"].astype(BF16)
    to = _tile(m, 512)
    orow = lambda n: pl.BlockSpec((to, n), lambda i: (i, 0))
    h1 = pl.pallas_call(
        _out_proj_kernel,
        grid=(m // to,),
        in_specs=[orow(d_model), orow(mla_width), orow(d_inner), _resident((1, mla_width)),
                  _resident((mla_width, d_model)), _resident((d_inner, d_model)), _resident((1, d_model))],
        out_specs=orow(d_model),
        out_shape=jax.ShapeDtypeStruct((m, d_model), F32),
        compiler_params=pltpu.CompilerParams(dimension_semantics=("arbitrary",),
                                             vmem_limit_bytes=VMEM_LIMIT_BYTES),
        name="out_proj",
    )(h.reshape(m, d_model), attn.reshape(m, mla_width), ssm.reshape(m, d_inner), _row(p["attn_out_norm_w"]),
      w_out[:mla_width], w_out[mla_width:], _row(p["post_mix_norm_w"]))

    tf_m = _tile(m, 512)
    tf_f = _tile(d_ff, 512)
    out = pl.pallas_call(
        _ffn_kernel,
        grid=(m // tf_m, d_ff // tf_f),
        in_specs=[pl.BlockSpec((tf_m, d_model), lambda i, f: (i, 0)),
                  _resident((1, d_model)),
                  pl.BlockSpec((d_model, tf_f), lambda i, f: (0, f)),
                  pl.BlockSpec((d_model, tf_f), lambda i, f: (0, f)),
                  pl.BlockSpec((tf_f, d_model), lambda i, f: (f, 0)),
                  _resident((1, d_model))],
        out_specs=pl.BlockSpec((tf_m, d_model), lambda i, f: (i, 0)),
        out_shape=jax.ShapeDtypeStruct((m, d_model), F32),
        scratch_shapes=[pltpu.VMEM((tf_m, d_model), BF16)],
        compiler_params=pltpu.CompilerParams(dimension_semantics=("arbitrary", "arbitrary"),
                                             vmem_limit_bytes=VMEM_LIMIT_BYTES),
        name="ffn",
    )(h1, _row(p["pre_ffn_norm_w"]), p["w_gate"].astype(BF16), p["w_up"].astype(BF16),
      p["w_down"].astype(BF16), _row(p["post_ffn_norm_w"]))
    return out.reshape(b, s, d_model)


def kernel(x, positions, w_in, q_norm_w, w_uq, kv_norm_w, w_ukv, conv_w, conv_b, dt_bias, a_log, d_skip,
           ssd_norm_w, attn_out_norm_w, w_out, pre_mix_norm_w, post_mix_norm_w, pre_ffn_norm_w,
           post_ffn_norm_w, w_gate, w_up, w_down):
    stacked = dict(w_in=w_in, q_norm_w=q_norm_w, w_uq=w_uq, kv_norm_w=kv_norm_w, w_ukv=w_ukv, conv_w=conv_w,
                   conv_b=conv_b, dt_bias=dt_bias, a_log=a_log, d_skip=d_skip, ssd_norm_w=ssd_norm_w,
                   attn_out_norm_w=attn_out_norm_w, w_out=w_out, pre_mix_norm_w=pre_mix_norm_w,
                   post_mix_norm_w=post_mix_norm_w, pre_ffn_norm_w=pre_ffn_norm_w,
                   post_ffn_norm_w=post_ffn_norm_w, w_gate=w_gate, w_up=w_up, w_down=w_down)
    b, s, _ = x.shape
    pos_col = positions.astype(F32).reshape(b * s, 1)
    h = x
    for l in range(w_in.shape[0]):
        h = _layer(h, pos_col, {k: v[l] for k, v in stacked.items()})
    return h
```

```python
import functools

import numpy as np
import jax
import jax.numpy as jnp
from jax import lax
from jax.experimental import pallas as pl
from jax.experimental.pallas import tpu as pltpu

F32 = jnp.float32
BF16 = jnp.bfloat16

V_HEAD_DIM = 128
QK_NOPE_DIM = 128
QK_ROPE_DIM = 64
QK_HEAD_DIM = QK_NOPE_DIM + QK_ROPE_DIM
Q_LORA_RANK = 512
KV_LORA_RANK = 512
ROPE_THETA = 10000.0
SSD_HEAD_DIM = 64
SSD_GROUPS = 2
SSD_STATE = 128
SSD_CONV = 4
SSD_CHUNK = 128
EPS = 1e-6

LANES = 128
SUBLANES = 8
VMEM_LIMIT_BYTES = 56 * 1024 * 1024

QK_PAD = 2 * LANES
ROPE_HALF = QK_ROPE_DIM // 2
LOG2_E = 1.4426950408889634
ONES_ROWS = 2 * SUBLANES
CONV_HALO = SUBLANES


def _rms(t, w):
    return t * lax.rsqrt(jnp.mean(t * t, axis=-1, keepdims=True) + EPS) * w


def _silu(t):
    h = 0.5 * t
    return h + h * jnp.tanh(h)


def _rope(t, cos, sin_signed):
    return t * cos + pltpu.roll(t, LANES // 2, axis=1) * sin_signed


def _ssd_chunk(k, ext_ref, hilo_ref, state_ref, z_c, dt_c, convw_ref, convb_ref, dtb_ref, aneg_ref,
               dskip_ref, nw_ref, tri_ref, e_ref, *, d_inner, n_heads):
    T, N, P, G = SSD_CHUNK, SSD_STATE, SSD_HEAD_DIM, SSD_GROUPS
    gw = d_inner // G

    r0 = CONV_HALO + k * T - (SSD_CONV - 1)
    conv = convb_ref[...] + convw_ref[0:1, :] * ext_ref[r0:r0 + T, :]
    for tap in range(1, SSD_CONV):
        conv = conv + convw_ref[tap:tap + 1, :] * ext_ref[r0 + tap:r0 + tap + T, :]
    xbc = _silu(conv)
    xs = xbc[:, :d_inner]
    bm = xbc[:, d_inner:d_inner + G * N]
    cm = xbc[:, d_inner + G * N:]

    dt_in = dt_c + dtb_ref[...]
    dt = jnp.maximum(dt_in, 0.0) + jnp.log1p(jnp.exp(-jnp.abs(dt_in)))
    a = dt * aneg_ref[...]
    a_hi = a.astype(BF16)
    a_r = a - a_hi.astype(F32)
    a_mid = a_r.astype(BF16)
    a_lo = (a_r - a_mid.astype(F32)).astype(BF16)
    tri_b = tri_ref[...]
    a_cum = (jnp.dot(tri_b, a_hi, preferred_element_type=F32) + jnp.dot(tri_b, a_mid, preferred_element_type=F32)
             + jnp.dot(tri_b, a_lo, preferred_element_type=F32))
    a_cum_t = a_cum.T
    a_last = a_cum[T - 1:T, :]
    ea = jnp.exp2(a_cum)
    ds = jnp.exp2(a_last - a_cum)

    stacked = jnp.concatenate([dt, ea, ds], axis=0)
    s_hi = stacked.astype(BF16)
    hilo_ref[k, :, :LANES] = s_hi
    hilo_ref[k, :, LANES:] = (stacked - s_hi.astype(F32)).astype(BF16)
    expanded = jnp.dot(hilo_ref[k], e_ref[...], preferred_element_type=F32)
    dt_x, ea_x, ds_x = expanded[0:T], expanded[T:2 * T], expanded[2 * T:3 * T]

    xdt = xs * dt_x
    xdt_b = xdt.astype(BF16)
    xw_b = (xdt * ds_x).astype(BF16)

    row = lax.broadcasted_iota(jnp.int32, (T, T), 0)
    col = lax.broadcasted_iota(jnp.int32, (T, T), 1)
    tri = col <= row
    lane = lax.broadcasted_iota(jnp.int32, (T, 2 * P), 1)
    heads_per_group = n_heads // G

    y_parts = []
    y_off_parts = []
    for g in range(G):
        cols = slice(g * gw, (g + 1) * gw)
        bm_g = bm[:, g * N:(g + 1) * N]
        cm_g = cm[:, g * N:(g + 1) * N].astype(BF16)
        cb = lax.dot_general(cm_g, bm_g.astype(BF16), (((1,), (1,)), ((), ())), preferred_element_type=F32)
        for j in range(heads_per_group // 2):
            c0 = g * gw + 2 * j * P
            xp = xdt_b[:, c0:c0 + 2 * P]
            zero = jnp.zeros_like(xp)
            part = None
            for i, x_half in enumerate((jnp.where(lane < P, xp, zero), jnp.where(lane >= P, xp, zero))):
                h = g * heads_per_group + 2 * j + i
                seg = a_cum[:, h:h + 1] - a_cum_t[h:h + 1, :]
                w = (cb * jnp.exp2(jnp.where(tri, seg, -jnp.inf))).astype(BF16)
                d = jnp.dot(w, x_half, preferred_element_type=F32)
                part = d if part is None else part + d
            y_parts.append(part)
        prev = state_ref[g]
        y_off_parts.append(jnp.dot(cm_g, prev.astype(BF16), preferred_element_type=F32))
        new = jnp.dot(bm_g.T.astype(BF16), xw_b[:, cols], preferred_element_type=F32)
        state_ref[g] = prev * ea_x[T - 1:T, cols] + new
    y_diag = jnp.concatenate(y_parts, axis=1)
    y_off = jnp.concatenate(y_off_parts, axis=1) * ea_x

    y = y_diag + y_off + xs * dskip_ref[...]
    gated = y * _silu(z_c)
    outs = []
    for g in range(G):
        gg = gated[:, g * gw:(g + 1) * gw]
        outs.append(gg * lax.rsqrt(jnp.mean(gg * gg, axis=-1, keepdims=True) + EPS))
    return (jnp.concatenate(outs, axis=1) * nw_ref[...]).astype(BF16)


def _in_ssd_kernel(x_ref, pos_ref, wpre_ref, wlat_ref, wz_ref, wxbc_ref, wmisc_ref, qnw_ref, wuq_ref, kvnw_ref,
                   wukv_ref, freq_ref, sgn_ref, convw_ref, convb_ref, dtb_ref, aneg_ref, dskip_ref,
                   nw_ref, tri_ref, e_ref, q_ref, kv_ref, kr_ref, ssm_ref, ext_ref, hilo_ref, state_ref,
                   *, n_heads, q_scale, seq_tiles, d_inner, n_ssd_heads):
    T = SSD_CHUNK
    tm = x_ref.shape[0]

    @pl.when(pl.program_id(0) % seq_tiles == 0)
    def _():
        ext_ref[0:CONV_HALO, :] = jnp.zeros((CONV_HALO, ext_ref.shape[1]), F32)
        state_ref[...] = jnp.zeros_like(state_ref)

    u = _rms(x_ref[...], wpre_ref[...]).astype(BF16)

    ext_ref[CONV_HALO:, :] = jnp.dot(u, wxbc_ref[...], preferred_element_type=F32)
    z = jnp.dot(u, wz_ref[...], preferred_element_type=F32)
    misc = jnp.dot(u, wmisc_ref[...], preferred_element_type=F32)
    dt_raw = misc[:, LANES:]
    for k in range(tm // T):
        ssm_ref[k * T:(k + 1) * T, :] = _ssd_chunk(
            k, ext_ref, hilo_ref, state_ref, z[k * T:(k + 1) * T], dt_raw[k * T:(k + 1) * T],
            convw_ref, convb_ref, dtb_ref, aneg_ref, dskip_ref, nw_ref, tri_ref, e_ref,
            d_inner=d_inner, n_heads=n_ssd_heads)
    ext_ref[0:CONV_HALO, :] = ext_ref[tm:tm + CONV_HALO, :]

    ang = pos_ref[...] * freq_ref[...]
    cos = jnp.cos(ang)
    sin = jnp.sin(ang) * sgn_ref[...]
    kr_ref[...] = _rope(misc[:, :LANES], cos, sin).astype(BF16)

    cqkv = jnp.dot(u, wlat_ref[...], preferred_element_type=F32)
    cqn = _rms(cqkv[:, :Q_LORA_RANK], qnw_ref[...]).astype(BF16)
    ckvn = _rms(cqkv[:, Q_LORA_RANK:], kvnw_ref[...]).astype(BF16)
    q = jnp.dot(cqn, wuq_ref[...], preferred_element_type=F32)
    cos_q = cos * q_scale
    sin_q = sin * q_scale
    for h in range(n_heads):
        base = h * QK_PAD
        q_ref[:, base:base + LANES] = (q[:, base:base + LANES] * q_scale).astype(BF16)
        q_ref[:, base + LANES:base + QK_PAD] = _rope(q[:, base + LANES:base + QK_PAD], cos_q, sin_q).astype(BF16)
    kv_ref[...] = jnp.dot(ckvn, wukv_ref[...], preferred_element_type=F32).astype(BF16)


def _mla_kernel(q_ref, kv_ref, kr_ref, o_ref, kc_ref, vt_ref, sa_ref, sb_ref, m_ref, acc_ref, *, tile, heads):
    qi = pl.program_id(2)
    n_tiles = vt_ref.shape[1]
    kv_w = 2 * LANES

    @pl.when(qi == 0)
    def _():
        for h in range(heads):
            kc_ref[h, :, :LANES] = kv_ref[0, :, h * kv_w:h * kv_w + LANES]
            kc_ref[h, :, LANES:] = kr_ref[0]
            for c in range(n_tiles):
                vt_ref[h, c, :V_HEAD_DIM] = kv_ref[0, c * tile:(c + 1) * tile, h * kv_w + LANES:(h + 1) * kv_w].T
                vt_ref[h, c, V_HEAD_DIM:] = jnp.ones((ONES_ROWS, tile), BF16)

    q_t = [q_ref[0, :, h * QK_PAD:(h + 1) * QK_PAD].T for h in range(heads)]
    m_ref[...] = jnp.full_like(m_ref, -jnp.inf)
    acc_ref[...] = jnp.zeros_like(acc_ref)

    def scores(j, s_ref):
        rows = pl.ds(pl.multiple_of(j * tile, tile), tile)
        for h in range(heads):
            s_ref[h] = jnp.dot(kc_ref[h, rows, :], q_t[h], preferred_element_type=F32)

    def consume(j, s_ref, masked):
        for h in range(heads):
            s = s_ref[h]
            if masked:
                k_idx = lax.broadcasted_iota(jnp.int32, s.shape, 0)
                q_idx = lax.broadcasted_iota(jnp.int32, s.shape, 1)
                s = jnp.where(k_idx <= q_idx, s, -jnp.inf)
            m_prev = m_ref[h]
            m_new = jnp.maximum(m_prev, jnp.max(s, axis=0, keepdims=True))
            alpha = jnp.exp2(m_prev - m_new)
            p = jnp.exp2(s - m_new)
            acc_ref[h] = alpha * acc_ref[h] + jnp.dot(vt_ref[h, j], p.astype(BF16), preferred_element_type=F32)
            m_ref[h] = m_new

    def pair(jj, carry):
        j = 2 * jj
        scores(j + 1, sb_ref)
        consume(j, sa_ref, False)
        scores(j + 2, sa_ref)
        consume(j + 1, sb_ref, False)
        return carry

    scores(0, sa_ref)
    lax.fori_loop(0, qi // 2, pair, 0)

    @pl.when(qi % 2 == 0)
    def _():
        consume(qi, sa_ref, True)

    @pl.when(qi % 2 == 1)
    def _():
        scores(qi, sb_ref)
        consume(qi - 1, sa_ref, False)
        consume(qi, sb_ref, True)

    for h in range(heads):
        out = acc_ref[h, :V_HEAD_DIM] / acc_ref[h, V_HEAD_DIM:V_HEAD_DIM + 1]
        o_ref[0, :, h * V_HEAD_DIM:(h + 1) * V_HEAD_DIM] = out.T.astype(o_ref.dtype)


def _out_proj_kernel(x_ref, attn_ref, ssm_ref, anw_ref, woa_ref, wos_ref, postw_ref, h_ref):
    attn_n = _rms(attn_ref[...].astype(F32), anw_ref[...]).astype(BF16)
    mix = jnp.dot(attn_n, woa_ref[...], preferred_element_type=F32)
    mix = mix + jnp.dot(ssm_ref[...], wos_ref[...], preferred_element_type=F32)
    h_ref[...] = x_ref[...] + _rms(mix, postw_ref[...])


def _ffn_kernel(h_ref, prew_ref, wg_ref, wu_ref, wd_ref, postw_ref, o_ref, v_ref):
    f = pl.program_id(1)

    @pl.when(f == 0)
    def _():
        v_ref[...] = _rms(h_ref[...], prew_ref[...]).astype(BF16)
        o_ref[...] = jnp.zeros_like(o_ref)

    v = v_ref[...]
    gate = jnp.dot(v, wg_ref[...], preferred_element_type=F32)
    up = jnp.dot(v, wu_ref[...], preferred_element_type=F32)
    act = (_silu(gate) * up).astype(BF16)
    o_ref[...] += jnp.dot(act, wd_ref[...], preferred_element_type=F32)

    @pl.when(f == pl.num_programs(1) - 1)
    def _():
        o_ref[...] = h_ref[...] + _rms(o_ref[...], postw_ref[...])


def _tile(n, want):
    t = min(n, want)
    assert n % t == 0, (n, t)
    return t


def _resident(shape):
    return pl.BlockSpec(shape, lambda *_: (0,) * len(shape), pipeline_mode=pl.Buffered(1))


def _row(w):
    return w.reshape(1, -1).astype(F32)


def _layer(h, pos_col, p):
    b, s, d_model = h.shape
    m = b * s
    n_mla_heads = p["w_uq"].shape[1] // QK_HEAD_DIM
    mla_width = n_mla_heads * V_HEAD_DIM
    d_inner = p["ssd_norm_w"].shape[0]
    n_ssd_heads = p["dt_bias"].shape[0]
    d_xbc = d_inner + 2 * SSD_GROUPS * SSD_STATE
    d_ff = p["w_gate"].shape[1]
    T = SSD_CHUNK
    assert n_ssd_heads * SSD_HEAD_DIM == d_inner and n_ssd_heads <= LANES

    w_in = p["w_in"]
    offs = np.cumsum([0, Q_LORA_RANK, KV_LORA_RANK, QK_ROPE_DIM, d_inner, d_xbc, n_ssd_heads])
    w_cq, w_ckv, w_kr, w_z, w_xbc, w_dt = [w_in[:, offs[i]:offs[i + 1]] for i in range(6)]
    zpad = lambda n: jnp.zeros((d_model, n), w_in.dtype)
    w_lat = w_in[:, :offs[2]].astype(BF16)
    w_misc = jnp.concatenate([w_kr[:, :ROPE_HALF], zpad(ROPE_HALF), w_kr[:, ROPE_HALF:], zpad(ROPE_HALF),
                              w_dt, zpad(LANES - n_ssd_heads)], axis=1).astype(BF16)

    wq3 = p["w_uq"].reshape(Q_LORA_RANK, n_mla_heads, QK_HEAD_DIM)
    zq = jnp.zeros((Q_LORA_RANK, n_mla_heads, ROPE_HALF), wq3.dtype)
    w_uq_r = jnp.concatenate([
        wq3[..., :QK_NOPE_DIM], wq3[..., QK_NOPE_DIM:QK_NOPE_DIM + ROPE_HALF], zq,
        wq3[..., QK_NOPE_DIM + ROPE_HALF:], zq], axis=-1).reshape(Q_LORA_RANK, n_mla_heads * QK_PAD).astype(BF16)
    w_ukv = p["w_ukv"].astype(BF16)

    inv_freq = ROPE_THETA ** (-jnp.arange(0, QK_ROPE_DIM, 2, dtype=F32) / QK_ROPE_DIM)
    z32 = jnp.zeros((ROPE_HALF,), F32)
    o32 = jnp.ones((ROPE_HALF,), F32)
    freq_tab = jnp.concatenate([inv_freq, z32, inv_freq, z32]).reshape(1, LANES)
    sgn_tab = jnp.concatenate([-o32, z32, o32, z32]).reshape(1, LANES)

    lane_pad = lambda v: jnp.pad(v.astype(F32), (0, LANES - n_ssd_heads)).reshape(1, LANES)
    a_neg = lane_pad(-jnp.exp(p["a_log"].astype(F32)) * LOG2_E)
    dt_bias = lane_pad(p["dt_bias"])
    d_skip_x = jnp.repeat(p["d_skip"].astype(F32), SSD_HEAD_DIM).reshape(1, d_inner)
    tri = jnp.tril(jnp.ones((T, T), BF16))
    expand = (jnp.arange(LANES)[:, None] == (jnp.arange(d_inner)[None, :] // SSD_HEAD_DIM)).astype(BF16)
    expand2 = jnp.concatenate([expand, expand], axis=0)

    tm = _tile(s, 512)
    assert tm % T == 0
    row_spec = lambda n: pl.BlockSpec((tm, n), lambda i: (i, 0))
    q, kv, kr, ssm = pl.pallas_call(
        functools.partial(_in_ssd_kernel, n_heads=n_mla_heads, q_scale=float(QK_HEAD_DIM) ** -0.5 * LOG2_E,
                          seq_tiles=s // tm, d_inner=d_inner, n_ssd_heads=n_ssd_heads),
        grid=(m // tm,),
        in_specs=[row_spec(d_model), row_spec(1), _resident((1, d_model)),
                  _resident((d_model, Q_LORA_RANK + KV_LORA_RANK)), _resident((d_model, d_inner)),
                  _resident((d_model, d_xbc)), _resident((d_model, 2 * LANES)),
                  _resident((1, Q_LORA_RANK)), _resident((Q_LORA_RANK, n_mla_heads * QK_PAD)),
                  _resident((1, KV_LORA_RANK)), _resident((KV_LORA_RANK, w_ukv.shape[1])),
                  _resident((1, LANES)), _resident((1, LANES)),
                  _resident((SSD_CONV, d_xbc)), _resident((1, d_xbc)),
                  _resident((1, LANES)), _resident((1, LANES)), _resident((1, d_inner)), _resident((1, d_inner)),
                  _resident((T, T)), _resident((2 * LANES, d_inner))],
        out_specs=[row_spec(n_mla_heads * QK_PAD), row_spec(w_ukv.shape[1]), row_spec(LANES), row_spec(d_inner)],
        out_shape=[jax.ShapeDtypeStruct((m, n_mla_heads * QK_PAD), BF16),
                   jax.ShapeDtypeStruct((m, w_ukv.shape[1]), BF16),
                   jax.ShapeDtypeStruct((m, LANES), BF16),
                   jax.ShapeDtypeStruct((m, d_inner), BF16)],
        scratch_shapes=[pltpu.VMEM((CONV_HALO + tm, d_xbc), F32),
                        pltpu.VMEM((tm // T, 3 * T, 2 * LANES), BF16),
                        pltpu.VMEM((SSD_GROUPS, SSD_STATE, d_inner // SSD_GROUPS), F32)],
        compiler_params=pltpu.CompilerParams(dimension_semantics=("arbitrary",),
                                             vmem_limit_bytes=VMEM_LIMIT_BYTES),
        name="in_ssd",
    )(h.reshape(m, d_model), pos_col, _row(p["pre_mix_norm_w"]), w_lat, w_z.astype(BF16), w_xbc.astype(BF16),
      w_misc, _row(p["q_norm_w"]), w_uq_r, _row(p["kv_norm_w"]), w_ukv, freq_tab, sgn_tab,
      p["conv_w"].astype(F32), _row(p["conv_b"]), dt_bias, a_neg, d_skip_x, _row(p["ssd_norm_w"]),
      tri, expand2)

    ta = _tile(s, 512)
    hp = next(c for c in (4, 2, 1) if n_mla_heads % c == 0)
    attn = pl.pallas_call(
        functools.partial(_mla_kernel, tile=ta, heads=hp),
        grid=(b, n_mla_heads // hp, s // ta),
        in_specs=[pl.BlockSpec((1, ta, hp * QK_PAD), lambda bi, hi, qi: (bi, qi, hi)),
                  pl.BlockSpec((1, s, hp * 2 * LANES), lambda bi, hi, qi: (bi, 0, hi)),
                  pl.BlockSpec((1, s, LANES), lambda bi, hi, qi: (bi, 0, 0))],
        out_specs=pl.BlockSpec((1, ta, hp * V_HEAD_DIM), lambda bi, hi, qi: (bi, qi, hi)),
        out_shape=jax.ShapeDtypeStruct((b, s, mla_width), BF16),
        scratch_shapes=[pltpu.VMEM((hp, s, QK_PAD), BF16),
                        pltpu.VMEM((hp, s // ta, V_HEAD_DIM + ONES_ROWS, ta), BF16),
                        pltpu.VMEM((hp, ta, ta), F32), pltpu.VMEM((hp, ta, ta), F32), pltpu.VMEM((hp, 1, ta), F32),
                        pltpu.VMEM((hp, V_HEAD_DIM + ONES_ROWS, ta), F32)],
        compiler_params=pltpu.CompilerParams(dimension_semantics=("arbitrary", "arbitrary", "arbitrary"),
                                             vmem_limit_bytes=VMEM_LIMIT_BYTES),
        name="mla",
    )(q.reshape(b, s, -1), kv.reshape(b, s, -1), kr.reshape(b, s, LANES))

    w_out = p["w_out"].astype(BF16)
    to = _tile(m, 512)
    orow = lambda n: pl.BlockSpec((to, n), lambda i: (i, 0))
    h1 = pl.pallas_call(
        _out_proj_kernel,
        grid=(m // to,),
        in_specs=[orow(d_model), orow(mla_width), orow(d_inner), _resident((1, mla_width)),
                  _resident((mla_width, d_model)), _resident((d_inner, d_model)), _resident((1, d_model))],
        out_specs=orow(d_model),
        out_shape=jax.ShapeDtypeStruct((m, d_model), F32),
        compiler_params=pltpu.CompilerParams(dimension_semantics=("arbitrary",),
                                             vmem_limit_bytes=VMEM_LIMIT_BYTES),
        name="out_proj",
    )(h.reshape(m, d_model), attn.reshape(m, mla_width), ssm, _row(p["attn_out_norm_w"]),
      w_out[:mla_width], w_out[mla_width:], _row(p["post_mix_norm_w"]))

    tf_m = _tile(m, 512)
    tf_f = _tile(d_ff, 512)
    out = pl.pallas_call(
        _ffn_kernel,
        grid=(m // tf_m, d_ff // tf_f),
        in_specs=[pl.BlockSpec((tf_m, d_model), lambda i, f: (i, 0)),
                  _resident((1, d_model)),
                  pl.BlockSpec((d_model, tf_f), lambda i, f: (0, f)),
                  pl.BlockSpec((d_model, tf_f), lambda i, f: (0, f)),
                  pl.BlockSpec((tf_f, d_model), lambda i, f: (f, 0)),
                  _resident((1, d_model))],
        out_specs=pl.BlockSpec((tf_m, d_model), lambda i, f: (i, 0)),
        out_shape=jax.ShapeDtypeStruct((m, d_model), F32),
        scratch_shapes=[pltpu.VMEM((tf_m, d_model), BF16)],
        compiler_params=pltpu.CompilerParams(dimension_semantics=("arbitrary", "arbitrary"),
                                             vmem_limit_bytes=VMEM_LIMIT_BYTES),
        name="ffn",
    )(h1, _row(p["pre_ffn_norm_w"]), p["w_gate"].astype(BF16), p["w_up"].astype(BF16),
      p["w_down"].astype(BF16), _row(p["post_ffn_norm_w"]))
    return out.reshape(b, s, d_model)


def kernel(x, positions, w_in, q_norm_w, w_uq, kv_norm_w, w_ukv, conv_w, conv_b, dt_bias, a_log, d_skip,
           ssd_norm_w, attn_out_norm_w, w_out, pre_mix_norm_w, post_mix_norm_w, pre_ffn_norm_w,
           post_ffn_norm_w, w_gate, w_up, w_down):
    stacked = dict(w_in=w_in, q_norm_w=q_norm_w, w_uq=w_uq, kv_norm_w=kv_norm_w, w_ukv=w_ukv, conv_w=conv_w,
                   conv_b=conv_b, dt_bias=dt_bias, a_log=a_log, d_skip=d_skip, ssd_norm_w=ssd_norm_w,
                   attn_out_norm_w=attn_out_norm_w, w_out=w_out, pre_mix_norm_w=pre_mix_norm_w,
                   post_mix_norm_w=post_mix_norm_w, pre_ffn_norm_w=pre_ffn_norm_w,
                   post_ffn_norm_w=post_ffn_norm_w, w_gate=w_gate, w_up=w_up, w_down=w_down)
    b, s, _ = x.shape
    pos_col = positions.astype(F32).reshape(b * s, 1)
    h = x
    for l in range(w_in.shape[0]):
        h = _layer(h, pos_col, {k: v[l] for k, v in stacked.items()})
    return h
```

```python
import functools

import numpy as np
import jax
import jax.numpy as jnp
from jax import lax
from jax.experimental import pallas as pl
from jax.experimental.pallas import tpu as pltpu

F32 = jnp.float32
BF16 = jnp.bfloat16

V_HEAD_DIM = 128
QK_NOPE_DIM = 128
QK_ROPE_DIM = 64
QK_HEAD_DIM = QK_NOPE_DIM + QK_ROPE_DIM
Q_LORA_RANK = 512
KV_LORA_RANK = 512
ROPE_THETA = 10000.0
SSD_HEAD_DIM = 64
SSD_GROUPS = 2
SSD_STATE = 128
SSD_CONV = 4
SSD_CHUNK = 128
EPS = 1e-6

LANES = 128
SUBLANES = 8
VMEM_LIMIT_BYTES = 56 * 1024 * 1024

QK_PAD = 2 * LANES
ROPE_HALF = QK_ROPE_DIM // 2
LOG2_E = 1.4426950408889634
ONES_ROWS = 2 * SUBLANES
CONV_HALO = SUBLANES


def _rms(t, w):
    return t * lax.rsqrt(jnp.mean(t * t, axis=-1, keepdims=True) + EPS) * w


def _silu(t):
    h = 0.5 * t
    return h + h * jnp.tanh(h)


def _rope(t, cos, sin_signed):
    return t * cos + pltpu.roll(t, LANES // 2, axis=1) * sin_signed


def _ssd_chunk(k, ext_ref, hilo_ref, state_ref, z_c, dt_c, convw_ref, convb_ref, dtb_ref, aneg_ref,
               dskip_ref, nw_ref, tri_ref, e_ref, *, d_inner, n_heads):
    T, N, P, G = SSD_CHUNK, SSD_STATE, SSD_HEAD_DIM, SSD_GROUPS
    gw = d_inner // G

    r0 = CONV_HALO + k * T - (SSD_CONV - 1)
    conv = convb_ref[...] + convw_ref[0:1, :] * ext_ref[r0:r0 + T, :]
    for tap in range(1, SSD_CONV):
        conv = conv + convw_ref[tap:tap + 1, :] * ext_ref[r0 + tap:r0 + tap + T, :]
    xbc = _silu(conv)
    xs = xbc[:, :d_inner]
    bm = xbc[:, d_inner:d_inner + G * N]
    cm = xbc[:, d_inner + G * N:]

    dt_in = dt_c + dtb_ref[...]
    dt = jnp.maximum(dt_in, 0.0) + jnp.log1p(jnp.exp(-jnp.abs(dt_in)))
    a = dt * aneg_ref[...]
    a_hi = a.astype(BF16)
    a_r = a - a_hi.astype(F32)
    a_mid = a_r.astype(BF16)
    a_lo = (a_r - a_mid.astype(F32)).astype(BF16)
    tri_b = tri_ref[...]
    a_cum = (jnp.dot(tri_b, a_hi, preferred_element_type=F32) + jnp.dot(tri_b, a_mid, preferred_element_type=F32)
             + jnp.dot(tri_b, a_lo, preferred_element_type=F32))
    a_cum_t = a_cum.T
    a_last = a_cum[T - 1:T, :]
    ea = jnp.exp2(a_cum)
    ds = jnp.exp2(a_last - a_cum)

    stacked = jnp.concatenate([dt, ea, ds], axis=0)
    s_hi = stacked.astype(BF16)
    hilo_ref[k, :, :LANES] = s_hi
    hilo_ref[k, :, LANES:] = (stacked - s_hi.astype(F32)).astype(BF16)
    expanded = jnp.dot(hilo_ref[k], e_ref[...], preferred_element_type=F32)
    dt_x, ea_x, ds_x = expanded[0:T], expanded[T:2 * T], expanded[2 * T:3 * T]

    xdt = xs * dt_x
    xdt_b = xdt.astype(BF16)
    xw_b = (xdt * ds_x).astype(BF16)

    row = lax.broadcasted_iota(jnp.int32, (T, T), 0)
    col = lax.broadcasted_iota(jnp.int32, (T, T), 1)
    tri = col <= row
    lane = lax.broadcasted_iota(jnp.int32, (T, 2 * P), 1)
    heads_per_group = n_heads // G

    y_parts = []
    y_off_parts = []
    for g in range(G):
        cols = slice(g * gw, (g + 1) * gw)
        bm_g = bm[:, g * N:(g + 1) * N]
        cm_g = cm[:, g * N:(g + 1) * N].astype(BF16)
        cb = lax.dot_general(cm_g, bm_g.astype(BF16), (((1,), (1,)), ((), ())), preferred_element_type=F32)
        for j in range(heads_per_group // 2):
            c0 = g * gw + 2 * j * P
            xp = xdt_b[:, c0:c0 + 2 * P]
            zero = jnp.zeros_like(xp)
            part = None
            for i, x_half in enumerate((jnp.where(lane < P, xp, zero), jnp.where(lane >= P, xp, zero))):
                h = g * heads_per_group + 2 * j + i
                seg = a_cum[:, h:h + 1] - a_cum_t[h:h + 1, :]
                w = (cb * jnp.exp2(jnp.where(tri, seg, -jnp.inf))).astype(BF16)
                d = jnp.dot(w, x_half, preferred_element_type=F32)
                part = d if part is None else part + d
            y_parts.append(part)
        prev = state_ref[g]
        y_off_parts.append(jnp.dot(cm_g, prev.astype(BF16), preferred_element_type=F32))
        new = jnp.dot(bm_g.T.astype(BF16), xw_b[:, cols], preferred_element_type=F32)
        state_ref[g] = prev * ea_x[T - 1:T, cols] + new
    y_diag = jnp.concatenate(y_parts, axis=1)
    y_off = jnp.concatenate(y_off_parts, axis=1) * ea_x

    y = y_diag + y_off + xs * dskip_ref[...]
    gated = y * _silu(z_c)
    outs = []
    for g in range(G):
        gg = gated[:, g * gw:(g + 1) * gw]
        outs.append(gg * lax.rsqrt(jnp.mean(gg * gg, axis=-1, keepdims=True) + EPS))
    return (jnp.concatenate(outs, axis=1) * nw_ref[...]).astype(BF16)


def _in_ssd_kernel(x_ref, pos_ref, wpre_ref, wlat_ref, wz_ref, wxbc_ref, wmisc_ref, qnw_ref, wuqt_ref, kvnw_ref,
                   wkn_ref, wvt_ref, freq_ref, sgn_ref, convw_ref, convb_ref, dtb_ref, aneg_ref, dskip_ref,
                   nw_ref, tri_ref, e_ref, qt_ref, kc_ref, vt_ref, ssm_ref, ext_ref, hilo_ref, state_ref,
                   *, n_heads, q_scale, seq_tiles, d_inner, n_ssd_heads):
    T = SSD_CHUNK
    tm = x_ref.shape[0]

    @pl.when(pl.program_id(0) % seq_tiles == 0)
    def _():
        ext_ref[0:CONV_HALO, :] = jnp.zeros((CONV_HALO, ext_ref.shape[1]), F32)
        state_ref[...] = jnp.zeros_like(state_ref)

    u = _rms(x_ref[...], wpre_ref[...]).astype(BF16)

    ext_ref[CONV_HALO:, :] = jnp.dot(u, wxbc_ref[...], preferred_element_type=F32)
    z = jnp.dot(u, wz_ref[...], preferred_element_type=F32)
    misc = jnp.dot(u, wmisc_ref[...], preferred_element_type=F32)
    dt_raw = misc[:, LANES:]
    for k in range(tm // T):
        ssm_ref[k * T:(k + 1) * T, :] = _ssd_chunk(
            k, ext_ref, hilo_ref, state_ref, z[k * T:(k + 1) * T], dt_raw[k * T:(k + 1) * T],
            convw_ref, convb_ref, dtb_ref, aneg_ref, dskip_ref, nw_ref, tri_ref, e_ref,
            d_inner=d_inner, n_heads=n_ssd_heads)
    ext_ref[0:CONV_HALO, :] = ext_ref[tm:tm + CONV_HALO, :]

    ang_t = freq_ref[...] * pos_ref[...]
    cos_t = jnp.cos(ang_t)
    sin_t = jnp.sin(ang_t) * sgn_ref[...]
    k_rope = _rope(misc[:, :LANES], cos_t.T, sin_t.T).astype(BF16)

    cqkv = jnp.dot(u, wlat_ref[...], preferred_element_type=F32)
    cqn = _rms(cqkv[:, :Q_LORA_RANK], qnw_ref[...]).astype(BF16)
    ckvn = _rms(cqkv[:, Q_LORA_RANK:], kvnw_ref[...]).astype(BF16)
    nt_dims = (((1,), (1,)), ((), ()))
    q_t = lax.dot_general(wuqt_ref[...], cqn, nt_dims, preferred_element_type=F32)
    cos_q = cos_t * q_scale
    sin_q = sin_t * q_scale
    half = LANES // 2
    for h in range(n_heads):
        base = h * QK_PAD
        qt_ref[0, base:base + LANES, :] = (q_t[base:base + LANES] * q_scale).astype(BF16)
        blk = q_t[base + LANES:base + QK_PAD]
        swapped = jnp.concatenate([blk[half:], blk[:half]], axis=0)
        qt_ref[0, base + LANES:base + QK_PAD, :] = (blk * cos_q + swapped * sin_q).astype(BF16)

    k_nope = jnp.dot(ckvn, wkn_ref[...], preferred_element_type=F32).astype(BF16)
    v_t = lax.dot_general(wvt_ref[...], ckvn, nt_dims, preferred_element_type=F32)
    v_rows = V_HEAD_DIM + ONES_ROWS
    for h in range(n_heads):
        kc_ref[:, h * QK_PAD:h * QK_PAD + LANES] = k_nope[:, h * LANES:(h + 1) * LANES]
        kc_ref[:, h * QK_PAD + LANES:(h + 1) * QK_PAD] = k_rope
        vt_ref[0, h * v_rows:h * v_rows + V_HEAD_DIM, :] = v_t[h * V_HEAD_DIM:(h + 1) * V_HEAD_DIM].astype(BF16)
        vt_ref[0, h * v_rows + V_HEAD_DIM:(h + 1) * v_rows, :] = jnp.ones((ONES_ROWS, tm), BF16)


def _mla_kernel(qt_ref, kc_ref, vt_ref, o_ref, sa_ref, sb_ref, m_ref, acc_ref, *, tile, heads):
    qi = pl.program_id(2)
    v_rows = V_HEAD_DIM + ONES_ROWS

    m_ref[...] = jnp.full_like(m_ref, -jnp.inf)
    acc_ref[...] = jnp.zeros_like(acc_ref)

    def scores(j, s_ref):
        rows = pl.ds(pl.multiple_of(j * tile, tile), tile)
        for h in range(heads):
            s_ref[h] = jnp.dot(kc_ref[0, rows, h * QK_PAD:(h + 1) * QK_PAD], qt_ref[0, h * QK_PAD:(h + 1) * QK_PAD, :],
                               preferred_element_type=F32)

    def consume(j, s_ref, masked):
        for h in range(heads):
            s = s_ref[h]
            if masked:
                k_idx = lax.broadcasted_iota(jnp.int32, s.shape, 0)
                q_idx = lax.broadcasted_iota(jnp.int32, s.shape, 1)
                s = jnp.where(k_idx <= q_idx, s, -jnp.inf)
            m_prev = m_ref[h]
            m_new = jnp.maximum(m_prev, jnp.max(s, axis=0, keepdims=True))
            alpha = jnp.exp2(m_prev - m_new)
            p = jnp.exp2(s - m_new)
            acc_ref[h] = alpha * acc_ref[h] + jnp.dot(vt_ref[0, j, h * v_rows:(h + 1) * v_rows, :], p.astype(BF16),
                                                      preferred_element_type=F32)
            m_ref[h] = m_new

    def pair(jj, carry):
        j = 2 * jj
        scores(j + 1, sb_ref)
        consume(j, sa_ref, False)
        scores(j + 2, sa_ref)
        consume(j + 1, sb_ref, False)
        return carry

    scores(0, sa_ref)
    lax.fori_loop(0, qi // 2, pair, 0)

    @pl.when(qi % 2 == 0)
    def _():
        consume(qi, sa_ref, True)

    @pl.when(qi % 2 == 1)
    def _():
        scores(qi, sb_ref)
        consume(qi - 1, sa_ref, False)
        consume(qi, sb_ref, True)

    for h in range(heads):
        out = acc_ref[h, :V_HEAD_DIM] / acc_ref[h, V_HEAD_DIM:V_HEAD_DIM + 1]
        o_ref[0, h * V_HEAD_DIM:(h + 1) * V_HEAD_DIM, :] = out.astype(o_ref.dtype)


def _out_proj_kernel(x_ref, attn_ref, ssm_ref, anw_ref, woa_ref, wos_ref, postw_ref, h_ref):
    a_t = attn_ref[0].astype(F32)
    scale = lax.rsqrt(jnp.mean(a_t * a_t, axis=0, keepdims=True) + EPS)
    attn_n = (a_t * scale * anw_ref[...]).astype(BF16).T
    mix = jnp.dot(attn_n, woa_ref[...], preferred_element_type=F32)
    mix = mix + jnp.dot(ssm_ref[...], wos_ref[...], preferred_element_type=F32)
    h_ref[...] = x_ref[...] + _rms(mix, postw_ref[...])


def _ffn_kernel(h_ref, prew_ref, wg_ref, wu_ref, wd_ref, postw_ref, o_ref, v_ref):
    f = pl.program_id(1)

    @pl.when(f == 0)
    def _():
        v_ref[...] = _rms(h_ref[...], prew_ref[...]).astype(BF16)
        o_ref[...] = jnp.zeros_like(o_ref)

    v = v_ref[...]
    gate = jnp.dot(v, wg_ref[...], preferred_element_type=F32)
    up = jnp.dot(v, wu_ref[...], preferred_element_type=F32)
    act = (_silu(gate) * up).astype(BF16)
    o_ref[...] += jnp.dot(act, wd_ref[...], preferred_element_type=F32)

    @pl.when(f == pl.num_programs(1) - 1)
    def _():
        o_ref[...] = h_ref[...] + _rms(o_ref[...], postw_ref[...])


def _tile(n, want):
    t = min(n, want)
    assert n % t == 0, (n, t)
    return t


def _resident(shape):
    return pl.BlockSpec(shape, lambda *_: (0,) * len(shape), pipeline_mode=pl.Buffered(1))


def _row(w):
    return w.reshape(1, -1).astype(F32)


def _layer(h, pos_row, p):
    b, s, d_model = h.shape
    m = b * s
    n_mla_heads = p["w_uq"].shape[1] // QK_HEAD_DIM
    mla_width = n_mla_heads * V_HEAD_DIM
    d_inner = p["ssd_norm_w"].shape[0]
    n_ssd_heads = p["dt_bias"].shape[0]
    d_xbc = d_inner + 2 * SSD_GROUPS * SSD_STATE
    d_ff = p["w_gate"].shape[1]
    T = SSD_CHUNK
    assert n_ssd_heads * SSD_HEAD_DIM == d_inner and n_ssd_heads <= LANES

    w_in = p["w_in"]
    offs = np.cumsum([0, Q_LORA_RANK, KV_LORA_RANK, QK_ROPE_DIM, d_inner, d_xbc, n_ssd_heads])
    w_cq, w_ckv, w_kr, w_z, w_xbc, w_dt = [w_in[:, offs[i]:offs[i + 1]] for i in range(6)]
    zpad = lambda n: jnp.zeros((d_model, n), w_in.dtype)
    w_lat = w_in[:, :offs[2]].astype(BF16)
    w_misc = jnp.concatenate([w_kr[:, :ROPE_HALF], zpad(ROPE_HALF), w_kr[:, ROPE_HALF:], zpad(ROPE_HALF),
                              w_dt, zpad(LANES - n_ssd_heads)], axis=1).astype(BF16)

    wq3 = p["w_uq"].reshape(Q_LORA_RANK, n_mla_heads, QK_HEAD_DIM)
    zq = jnp.zeros((Q_LORA_RANK, n_mla_heads, ROPE_HALF), wq3.dtype)
    w_uq_r = jnp.concatenate([
        wq3[..., :QK_NOPE_DIM], wq3[..., QK_NOPE_DIM:QK_NOPE_DIM + ROPE_HALF], zq,
        wq3[..., QK_NOPE_DIM + ROPE_HALF:], zq], axis=-1).reshape(Q_LORA_RANK, n_mla_heads * QK_PAD).astype(BF16)
    w_uq_t = w_uq_r.T
    wkv3 = p["w_ukv"].astype(BF16).reshape(KV_LORA_RANK, n_mla_heads, QK_NOPE_DIM + V_HEAD_DIM)
    w_kn = wkv3[..., :QK_NOPE_DIM].reshape(KV_LORA_RANK, n_mla_heads * QK_NOPE_DIM)
    w_v_t = wkv3[..., QK_NOPE_DIM:].reshape(KV_LORA_RANK, mla_width).T

    inv_freq = ROPE_THETA ** (-jnp.arange(0, QK_ROPE_DIM, 2, dtype=F32) / QK_ROPE_DIM)
    z32 = jnp.zeros((ROPE_HALF,), F32)
    o32 = jnp.ones((ROPE_HALF,), F32)
    freq_tab = jnp.concatenate([inv_freq, z32, inv_freq, z32]).reshape(LANES, 1)
    sgn_tab = jnp.concatenate([-o32, z32, o32, z32]).reshape(LANES, 1)

    lane_pad = lambda v: jnp.pad(v.astype(F32), (0, LANES - n_ssd_heads)).reshape(1, LANES)
    a_neg = lane_pad(-jnp.exp(p["a_log"].astype(F32)) * LOG2_E)
    dt_bias = lane_pad(p["dt_bias"])
    d_skip_x = jnp.repeat(p["d_skip"].astype(F32), SSD_HEAD_DIM).reshape(1, d_inner)
    tri = jnp.tril(jnp.ones((T, T), BF16))
    expand = (jnp.arange(LANES)[:, None] == (jnp.arange(d_inner)[None, :] // SSD_HEAD_DIM)).astype(BF16)
    expand2 = jnp.concatenate([expand, expand], axis=0)

    tm = _tile(s, 512)
    assert tm % T == 0
    n_tiles = s // tm
    v_rows = V_HEAD_DIM + ONES_ROWS
    row_spec = lambda n: pl.BlockSpec((tm, n), lambda i: (i, 0))
    col_spec = lambda n: pl.BlockSpec((1, n, tm), lambda i: (i, 0, 0))
    q_t, kc, v_t, ssm = pl.pallas_call(
        functools.partial(_in_ssd_kernel, n_heads=n_mla_heads, q_scale=float(QK_HEAD_DIM) ** -0.5 * LOG2_E,
                          seq_tiles=n_tiles, d_inner=d_inner, n_ssd_heads=n_ssd_heads),
        grid=(m // tm,),
        in_specs=[row_spec(d_model), pl.BlockSpec((1, tm), lambda i: (0, i)), _resident((1, d_model)),
                  _resident((d_model, Q_LORA_RANK + KV_LORA_RANK)), _resident((d_model, d_inner)),
                  _resident((d_model, d_xbc)), _resident((d_model, 2 * LANES)),
                  _resident((1, Q_LORA_RANK)), _resident((n_mla_heads * QK_PAD, Q_LORA_RANK)),
                  _resident((1, KV_LORA_RANK)), _resident((KV_LORA_RANK, n_mla_heads * QK_NOPE_DIM)),
                  _resident((mla_width, KV_LORA_RANK)),
                  _resident((LANES, 1)), _resident((LANES, 1)),
                  _resident((SSD_CONV, d_xbc)), _resident((1, d_xbc)),
                  _resident((1, LANES)), _resident((1, LANES)), _resident((1, d_inner)), _resident((1, d_inner)),
                  _resident((T, T)), _resident((2 * LANES, d_inner))],
        out_specs=[col_spec(n_mla_heads * QK_PAD), row_spec(n_mla_heads * QK_PAD), col_spec(n_mla_heads * v_rows),
                   row_spec(d_inner)],
        out_shape=[jax.ShapeDtypeStruct((m // tm, n_mla_heads * QK_PAD, tm), BF16),
                   jax.ShapeDtypeStruct((m, n_mla_heads * QK_PAD), BF16),
                   jax.ShapeDtypeStruct((m // tm, n_mla_heads * v_rows, tm), BF16),
                   jax.ShapeDtypeStruct((m, d_inner), BF16)],
        scratch_shapes=[pltpu.VMEM((CONV_HALO + tm, d_xbc), F32),
                        pltpu.VMEM((tm // T, 3 * T, 2 * LANES), BF16),
                        pltpu.VMEM((SSD_GROUPS, SSD_STATE, d_inner // SSD_GROUPS), F32)],
        compiler_params=pltpu.CompilerParams(dimension_semantics=("arbitrary",),
                                             vmem_limit_bytes=VMEM_LIMIT_BYTES),
        name="in_ssd",
    )(h.reshape(m, d_model), pos_row, _row(p["pre_mix_norm_w"]), w_lat, w_z.astype(BF16), w_xbc.astype(BF16),
      w_misc, _row(p["q_norm_w"]), w_uq_t, _row(p["kv_norm_w"]), w_kn, w_v_t, freq_tab, sgn_tab,
      p["conv_w"].astype(F32), _row(p["conv_b"]), dt_bias, a_neg, d_skip_x, _row(p["ssd_norm_w"]),
      tri, expand2)

    hp = next(c for c in (4, 2, 1) if n_mla_heads % c == 0)
    attn_t = pl.pallas_call(
        functools.partial(_mla_kernel, tile=tm, heads=hp),
        grid=(b, n_mla_heads // hp, n_tiles),
        in_specs=[pl.BlockSpec((1, hp * QK_PAD, tm), lambda bi, hi, qi: (bi * n_tiles + qi, hi, 0)),
                  pl.BlockSpec((1, s, hp * QK_PAD), lambda bi, hi, qi: (bi, 0, hi)),
                  pl.BlockSpec((1, n_tiles, hp * v_rows, tm), lambda bi, hi, qi: (bi, 0, hi, 0))],
        out_specs=pl.BlockSpec((1, hp * V_HEAD_DIM, tm), lambda bi, hi, qi: (bi * n_tiles + qi, hi, 0)),
        out_shape=jax.ShapeDtypeStruct((m // tm, mla_width, tm), BF16),
        scratch_shapes=[pltpu.VMEM((hp, tm, tm), F32), pltpu.VMEM((hp, tm, tm), F32), pltpu.VMEM((hp, 1, tm), F32),
                        pltpu.VMEM((hp, v_rows, tm), F32)],
        compiler_params=pltpu.CompilerParams(dimension_semantics=("arbitrary", "arbitrary", "arbitrary"),
                                             vmem_limit_bytes=VMEM_LIMIT_BYTES),
        name="mla",
    )(q_t, kc.reshape(b, s, -1), v_t.reshape(b, n_tiles, n_mla_heads * v_rows, tm))

    w_out = p["w_out"].astype(BF16)
    orow = lambda n: pl.BlockSpec((tm, n), lambda i: (i, 0))
    h1 = pl.pallas_call(
        _out_proj_kernel,
        grid=(m // tm,),
        in_specs=[orow(d_model), col_spec(mla_width), orow(d_inner), _resident((mla_width, 1)),
                  _resident((mla_width, d_model)), _resident((d_inner, d_model)), _resident((1, d_model))],
        out_specs=orow(d_model),
        out_shape=jax.ShapeDtypeStruct((m, d_model), F32),
        compiler_params=pltpu.CompilerParams(dimension_semantics=("arbitrary",),
                                             vmem_limit_bytes=VMEM_LIMIT_BYTES),
        name="out_proj",
    )(h.reshape(m, d_model), attn_t, ssm, p["attn_out_norm_w"].astype(F32).reshape(mla_width, 1),
      w_out[:mla_width], w_out[mla_width:], _row(p["post_mix_norm_w"]))

    tf_m = _tile(m, 512)
    tf_f = _tile(d_ff, 512)
    out = pl.pallas_call(
        _ffn_kernel,
        grid=(m // tf_m, d_ff // tf_f),
        in_specs=[pl.BlockSpec((tf_m, d_model), lambda i, f: (i, 0)),
                  _resident((1, d_model)),
                  pl.BlockSpec((d_model, tf_f), lambda i, f: (0, f)),
                  pl.BlockSpec((d_model, tf_f), lambda i, f: (0, f)),
                  pl.BlockSpec((tf_f, d_model), lambda i, f: (f, 0)),
                  _resident((1, d_model))],
        out_specs=pl.BlockSpec((tf_m, d_model), lambda i, f: (i, 0)),
        out_shape=jax.ShapeDtypeStruct((m, d_model), F32),
        scratch_shapes=[pltpu.VMEM((tf_m, d_model), BF16)],
        compiler_params=pltpu.CompilerParams(dimension_semantics=("arbitrary", "arbitrary"),
                                             vmem_limit_bytes=VMEM_LIMIT_BYTES),
        name="ffn",
    )(h1, _row(p["pre_ffn_norm_w"]), p["w_gate"].astype(BF16), p["w_up"].astype(BF16),
      p["w_down"].astype(BF16), _row(p["post_ffn_norm_w"]))
    return out.reshape(b, s, d_model)


def kernel(x, positions, w_in, q_norm_w, w_uq, kv_norm_w, w_ukv, conv_w, conv_b, dt_bias, a_log, d_skip,
           ssd_norm_w, attn_out_norm_w, w_out, pre_mix_norm_w, post_mix_norm_w, pre_ffn_norm_w,
           post_ffn_norm_w, w_gate, w_up, w_down):
    stacked = dict(w_in=w_in, q_norm_w=q_norm_w, w_uq=w_uq, kv_norm_w=kv_norm_w, w_ukv=w_ukv, conv_w=conv_w,
                   conv_b=conv_b, dt_bias=dt_bias, a_log=a_log, d_skip=d_skip, ssd_norm_w=ssd_norm_w,
                   attn_out_norm_w=attn_out_norm_w, w_out=w_out, pre_mix_norm_w=pre_mix_norm_w,
                   post_mix_norm_w=post_mix_norm_w, pre_ffn_norm_w=pre_ffn_norm_w,
                   post_ffn_norm_w=post_ffn_norm_w, w_gate=w_gate, w_up=w_up, w_down=w_down)
    b, s, _ = x.shape
    pos_row = positions.astype(F32).reshape(1, b * s)
    h = x
    for l in range(w_in.shape[0]):
        h = _layer(h, pos_row, {k: v[l] for k, v in stacked.items()})
    return h
```

```python
import functools

import numpy as np
import jax
import jax.numpy as jnp
from jax import lax
from jax.experimental import pallas as pl
from jax.experimental.pallas import tpu as pltpu

F32 = jnp.float32
BF16 = jnp.bfloat16

V_HEAD_DIM = 128
QK_NOPE_DIM = 128
QK_ROPE_DIM = 64
QK_HEAD_DIM = QK_NOPE_DIM + QK_ROPE_DIM
Q_LORA_RANK = 512
KV_LORA_RANK = 512
ROPE_THETA = 10000.0
SSD_HEAD_DIM = 64
SSD_GROUPS = 2
SSD_STATE = 128
SSD_CONV = 4
SSD_CHUNK = 128
EPS = 1e-6

LANES = 128
SUBLANES = 8
VMEM_LIMIT_BYTES = 56 * 1024 * 1024

QK_PAD = 2 * LANES
ROPE_HALF = QK_ROPE_DIM // 2
LOG2_E = 1.4426950408889634
ONES_ROWS = 2 * SUBLANES
CONV_HALO = SUBLANES


def _rms(t, w):
    return t * lax.rsqrt(jnp.mean(t * t, axis=-1, keepdims=True) + EPS) * w


def _silu(t):
    h = 0.5 * t
    return h + h * jnp.tanh(h)


def _rope(t, cos, sin_signed):
    return t * cos + pltpu.roll(t, LANES // 2, axis=1) * sin_signed


def _ssd_chunk(k, ext_ref, hilo_ref, state_ref, z_c, dt_c, convw_ref, convb_ref, dtb_ref, aneg_ref,
               dskip_ref, nw_ref, tri_ref, e_ref, *, d_inner, n_heads):
    T, N, P, G = SSD_CHUNK, SSD_STATE, SSD_HEAD_DIM, SSD_GROUPS
    gw = d_inner // G

    r0 = CONV_HALO + k * T - (SSD_CONV - 1)
    conv = convb_ref[...] + convw_ref[0:1, :] * ext_ref[r0:r0 + T, :]
    for tap in range(1, SSD_CONV):
        conv = conv + convw_ref[tap:tap + 1, :] * ext_ref[r0 + tap:r0 + tap + T, :]
    xbc = _silu(conv)
    xs = xbc[:, :d_inner]
    bm = xbc[:, d_inner:d_inner + G * N]
    cm = xbc[:, d_inner + G * N:]

    dt_in = dt_c + dtb_ref[...]
    dt = jnp.maximum(dt_in, 0.0) + jnp.log1p(jnp.exp(-jnp.abs(dt_in)))
    a = dt * aneg_ref[...]
    a_hi = a.astype(BF16)
    a_r = a - a_hi.astype(F32)
    a_mid = a_r.astype(BF16)
    a_lo = (a_r - a_mid.astype(F32)).astype(BF16)
    tri_b = tri_ref[...]
    a_cum = (jnp.dot(tri_b, a_hi, preferred_element_type=F32) + jnp.dot(tri_b, a_mid, preferred_element_type=F32)
             + jnp.dot(tri_b, a_lo, preferred_element_type=F32))
    a_cum_t = a_cum.T
    a_last = a_cum[T - 1:T, :]
    ea = jnp.exp2(a_cum)
    ds = jnp.exp2(a_last - a_cum)

    stacked = jnp.concatenate([dt, ea, ds], axis=0)
    s_hi = stacked.astype(BF16)
    hilo_ref[k, :, :LANES] = s_hi
    hilo_ref[k, :, LANES:] = (stacked - s_hi.astype(F32)).astype(BF16)
    expanded = jnp.dot(hilo_ref[k], e_ref[...], preferred_element_type=F32)
    dt_x, ea_x, ds_x = expanded[0:T], expanded[T:2 * T], expanded[2 * T:3 * T]

    xdt = xs * dt_x
    xdt_b = xdt.astype(BF16)
    xw_b = (xdt * ds_x).astype(BF16)

    row = lax.broadcasted_iota(jnp.int32, (T, T), 0)
    col = lax.broadcasted_iota(jnp.int32, (T, T), 1)
    tri = col <= row
    lane = lax.broadcasted_iota(jnp.int32, (T, 2 * P), 1)
    heads_per_group = n_heads // G

    y_parts = []
    y_off_parts = []
    for g in range(G):
        cols = slice(g * gw, (g + 1) * gw)
        bm_g = bm[:, g * N:(g + 1) * N]
        cm_g = cm[:, g * N:(g + 1) * N].astype(BF16)
        cb = lax.dot_general(cm_g, bm_g.astype(BF16), (((1,), (1,)), ((), ())), preferred_element_type=F32)
        for j in range(heads_per_group // 2):
            c0 = g * gw + 2 * j * P
            xp = xdt_b[:, c0:c0 + 2 * P]
            zero = jnp.zeros_like(xp)
            part = None
            for i, x_half in enumerate((jnp.where(lane < P, xp, zero), jnp.where(lane >= P, xp, zero))):
                h = g * heads_per_group + 2 * j + i
                seg = a_cum[:, h:h + 1] - a_cum_t[h:h + 1, :]
                w = (cb * jnp.exp2(jnp.where(tri, seg, -jnp.inf))).astype(BF16)
                d = jnp.dot(w, x_half, preferred_element_type=F32)
                part = d if part is None else part + d
            y_parts.append(part)
        prev = state_ref[g]
        y_off_parts.append(jnp.dot(cm_g, prev.astype(BF16), preferred_element_type=F32))
        new = jnp.dot(bm_g.T.astype(BF16), xw_b[:, cols], preferred_element_type=F32)
        state_ref[g] = prev * ea_x[T - 1:T, cols] + new
    y_diag = jnp.concatenate(y_parts, axis=1)
    y_off = jnp.concatenate(y_off_parts, axis=1) * ea_x

    y = y_diag + y_off + xs * dskip_ref[...]
    gated = y * _silu(z_c)
    outs = []
    for g in range(G):
        gg = gated[:, g * gw:(g + 1) * gw]
        outs.append(gg * lax.rsqrt(jnp.mean(gg * gg, axis=-1, keepdims=True) + EPS))
    return (jnp.concatenate(outs, axis=1) * nw_ref[...]).astype(BF16)


def _in_ssd_kernel(x_ref, pos_ref, wpre_ref, wlat_ref, wz_ref, wxbc_ref, wmisc_ref, qnw_ref, wuqt_ref, kvnw_ref,
                   wkn_ref, wvt_ref, freq_ref, sgn_ref, convw_ref, convb_ref, dtb_ref, aneg_ref, dskip_ref,
                   nw_ref, tri_ref, e_ref, qt_ref, kc_ref, vt_ref, ssm_ref, ext_ref, hilo_ref, state_ref,
                   *, n_heads, q_scale, seq_tiles, d_inner, n_ssd_heads):
    T = SSD_CHUNK
    tm = x_ref.shape[0]

    @pl.when(pl.program_id(0) % seq_tiles == 0)
    def _():
        ext_ref[0:CONV_HALO, :] = jnp.zeros((CONV_HALO, ext_ref.shape[1]), F32)
        state_ref[...] = jnp.zeros_like(state_ref)

    u = _rms(x_ref[...], wpre_ref[...]).astype(BF16)

    ext_ref[CONV_HALO:, :] = jnp.dot(u, wxbc_ref[...], preferred_element_type=F32)
    z = jnp.dot(u, wz_ref[...], preferred_element_type=F32)
    misc = jnp.dot(u, wmisc_ref[...], preferred_element_type=F32)
    dt_raw = misc[:, LANES:]
    for k in range(tm // T):
        ssm_ref[k * T:(k + 1) * T, :] = _ssd_chunk(
            k, ext_ref, hilo_ref, state_ref, z[k * T:(k + 1) * T], dt_raw[k * T:(k + 1) * T],
            convw_ref, convb_ref, dtb_ref, aneg_ref, dskip_ref, nw_ref, tri_ref, e_ref,
            d_inner=d_inner, n_heads=n_ssd_heads)
    ext_ref[0:CONV_HALO, :] = ext_ref[tm:tm + CONV_HALO, :]

    ang_t = freq_ref[...] * pos_ref[...]
    cos_t = jnp.cos(ang_t)
    sin_t = jnp.sin(ang_t) * sgn_ref[...]
    k_rope = _rope(misc[:, :LANES], cos_t.T, sin_t.T).astype(BF16)

    cqkv = jnp.dot(u, wlat_ref[...], preferred_element_type=F32)
    cqn = _rms(cqkv[:, :Q_LORA_RANK], qnw_ref[...]).astype(BF16)
    ckvn = _rms(cqkv[:, Q_LORA_RANK:], kvnw_ref[...]).astype(BF16)
    nt_dims = (((1,), (1,)), ((), ()))
    q_t = lax.dot_general(wuqt_ref[...], cqn, nt_dims, preferred_element_type=F32)
    cos_q = cos_t * q_scale
    sin_q = sin_t * q_scale
    half = LANES // 2
    for h in range(n_heads):
        base = h * QK_PAD
        qt_ref[0, base:base + LANES, :] = (q_t[base:base + LANES] * q_scale).astype(BF16)
        blk = q_t[base + LANES:base + QK_PAD]
        swapped = jnp.concatenate([blk[half:], blk[:half]], axis=0)
        qt_ref[0, base + LANES:base + QK_PAD, :] = (blk * cos_q + swapped * sin_q).astype(BF16)

    k_nope = jnp.dot(ckvn, wkn_ref[...], preferred_element_type=F32).astype(BF16)
    v_t = lax.dot_general(wvt_ref[...], ckvn, nt_dims, preferred_element_type=F32)
    v_rows = V_HEAD_DIM + ONES_ROWS
    for h in range(n_heads):
        kc_ref[:, h * QK_PAD:h * QK_PAD + LANES] = k_nope[:, h * LANES:(h + 1) * LANES]
        kc_ref[:, h * QK_PAD + LANES:(h + 1) * QK_PAD] = k_rope
        vt_ref[0, h * v_rows:h * v_rows + V_HEAD_DIM, :] = v_t[h * V_HEAD_DIM:(h + 1) * V_HEAD_DIM].astype(BF16)
        vt_ref[0, h * v_rows + V_HEAD_DIM:(h + 1) * v_rows, :] = jnp.ones((ONES_ROWS, tm), BF16)


def _mla_kernel(qt_ref, kc_ref, vt_ref, o_ref, sa_ref, sb_ref, m_ref, acc_ref, *, tile, heads):
    qi = pl.program_id(2)
    v_rows = V_HEAD_DIM + ONES_ROWS

    m_ref[...] = jnp.full_like(m_ref, -jnp.inf)
    acc_ref[...] = jnp.zeros_like(acc_ref)

    def scores(j, s_ref):
        rows = pl.ds(pl.multiple_of(j * tile, tile), tile)
        for h in range(heads):
            s_ref[h] = jnp.dot(kc_ref[0, rows, h * QK_PAD:(h + 1) * QK_PAD], qt_ref[0, h * QK_PAD:(h + 1) * QK_PAD, :],
                               preferred_element_type=F32)

    def consume(j, s_ref, masked):
        for h in range(heads):
            s = s_ref[h]
            if masked:
                k_idx = lax.broadcasted_iota(jnp.int32, s.shape, 0)
                q_idx = lax.broadcasted_iota(jnp.int32, s.shape, 1)
                s = jnp.where(k_idx <= q_idx, s, -jnp.inf)
            m_prev = m_ref[h]
            m_new = jnp.maximum(m_prev, jnp.max(s, axis=0, keepdims=True))
            alpha = jnp.exp2(m_prev - m_new)
            p = jnp.exp2(s - m_new)
            acc_ref[h] = alpha * acc_ref[h] + jnp.dot(vt_ref[0, j, h * v_rows:(h + 1) * v_rows, :], p.astype(BF16),
                                                      preferred_element_type=F32)
            m_ref[h] = m_new

    def pair(jj, carry):
        j = 2 * jj
        scores(j + 1, sb_ref)
        consume(j, sa_ref, False)
        scores(j + 2, sa_ref)
        consume(j + 1, sb_ref, False)
        return carry

    scores(0, sa_ref)
    lax.fori_loop(0, qi // 2, pair, 0)

    @pl.when(qi % 2 == 0)
    def _():
        consume(qi, sa_ref, True)

    @pl.when(qi % 2 == 1)
    def _():
        scores(qi, sb_ref)
        consume(qi - 1, sa_ref, False)
        consume(qi, sb_ref, True)

    for h in range(heads):
        out = acc_ref[h, :V_HEAD_DIM] / acc_ref[h, V_HEAD_DIM:V_HEAD_DIM + 1]
        o_ref[0, h * V_HEAD_DIM:(h + 1) * V_HEAD_DIM, :] = out.astype(o_ref.dtype)


def _out_proj_kernel(x_ref, attn_ref, ssm_ref, anw_ref, woa_ref, wos_ref, postw_ref, h_ref):
    a_t = attn_ref[0].astype(F32)
    scale = lax.rsqrt(jnp.mean(a_t * a_t, axis=0, keepdims=True) + EPS)
    attn_n = (a_t * scale * anw_ref[...]).astype(BF16).T
    mix = jnp.dot(attn_n, woa_ref[...], preferred_element_type=F32)
    mix = mix + jnp.dot(ssm_ref[...], wos_ref[...], preferred_element_type=F32)
    h_ref[...] = x_ref[...] + _rms(mix, postw_ref[...])


def _ffn_kernel(h_ref, prew_ref, wg_ref, wu_ref, wd_ref, postw_ref, o_ref, v_ref):
    f = pl.program_id(1)
    last = pl.num_programs(1) - 1
    tm = h_ref.shape[0]
    row_halves = (slice(0, tm // 2), slice(tm // 2, tm))

    def prenorm(rows):
        v_ref[rows, :] = _rms(h_ref[rows, :], prew_ref[...]).astype(BF16)
        o_ref[rows, :] = jnp.zeros((tm // 2, o_ref.shape[1]), F32)

    def swiglu(rows):
        v = v_ref[rows, :]
        half = wg_ref.shape[1] // 2
        acts = []
        for c in range(2):
            cols = slice(c * half, (c + 1) * half)
            gate = jnp.dot(v, wg_ref[:, cols], preferred_element_type=F32)
            up = jnp.dot(v, wu_ref[:, cols], preferred_element_type=F32)
            acts.append((_silu(gate) * up).astype(BF16))
        o_ref[rows, :] += jnp.dot(jnp.concatenate(acts, axis=1), wd_ref[...], preferred_element_type=F32)

    def finalize(rows):
        o_ref[rows, :] = h_ref[rows, :] + _rms(o_ref[rows, :], postw_ref[...])

    @pl.when(f == 0)
    def _():
        prenorm(row_halves[0])
        swiglu(row_halves[0])
        prenorm(row_halves[1])
        swiglu(row_halves[1])

    @pl.when((f > 0) & (f < last))
    def _():
        swiglu(slice(0, tm))

    @pl.when(f == last)
    def _():
        swiglu(row_halves[0])
        finalize(row_halves[0])
        swiglu(row_halves[1])
        finalize(row_halves[1])


def _tile(n, want):
    t = min(n, want)
    assert n % t == 0, (n, t)
    return t


def _resident(shape):
    return pl.BlockSpec(shape, lambda *_: (0,) * len(shape), pipeline_mode=pl.Buffered(1))


def _row(w):
    return w.reshape(1, -1).astype(F32)


def _layer(h, pos_row, p):
    b, s, d_model = h.shape
    m = b * s
    n_mla_heads = p["w_uq"].shape[1] // QK_HEAD_DIM
    mla_width = n_mla_heads * V_HEAD_DIM
    d_inner = p["ssd_norm_w"].shape[0]
    n_ssd_heads = p["dt_bias"].shape[0]
    d_xbc = d_inner + 2 * SSD_GROUPS * SSD_STATE
    d_ff = p["w_gate"].shape[1]
    T = SSD_CHUNK
    assert n_ssd_heads * SSD_HEAD_DIM == d_inner and n_ssd_heads <= LANES

    w_in = p["w_in"].astype(BF16)
    offs = np.cumsum([0, Q_LORA_RANK, KV_LORA_RANK, QK_ROPE_DIM, d_inner, d_xbc, n_ssd_heads])
    w_cq, w_ckv, w_kr, w_z, w_xbc, w_dt = [w_in[:, offs[i]:offs[i + 1]] for i in range(6)]
    zpad = lambda n: jnp.zeros((d_model, n), BF16)
    w_lat = w_in[:, :offs[2]]
    w_misc = jnp.concatenate([w_kr[:, :ROPE_HALF], zpad(ROPE_HALF), w_kr[:, ROPE_HALF:], zpad(ROPE_HALF),
                              w_dt, zpad(LANES - n_ssd_heads)], axis=1)

    wq3 = p["w_uq"].reshape(Q_LORA_RANK, n_mla_heads, QK_HEAD_DIM)
    zq = jnp.zeros((Q_LORA_RANK, n_mla_heads, ROPE_HALF), wq3.dtype)
    w_uq_r = jnp.concatenate([
        wq3[..., :QK_NOPE_DIM], wq3[..., QK_NOPE_DIM:QK_NOPE_DIM + ROPE_HALF], zq,
        wq3[..., QK_NOPE_DIM + ROPE_HALF:], zq], axis=-1).reshape(Q_LORA_RANK, n_mla_heads * QK_PAD).astype(BF16)
    w_uq_t = w_uq_r.T
    wkv3 = p["w_ukv"].astype(BF16).reshape(KV_LORA_RANK, n_mla_heads, QK_NOPE_DIM + V_HEAD_DIM)
    w_kn = wkv3[..., :QK_NOPE_DIM].reshape(KV_LORA_RANK, n_mla_heads * QK_NOPE_DIM)
    w_v_t = wkv3[..., QK_NOPE_DIM:].reshape(KV_LORA_RANK, mla_width).T

    inv_freq = ROPE_THETA ** (-jnp.arange(0, QK_ROPE_DIM, 2, dtype=F32) / QK_ROPE_DIM)
    z32 = jnp.zeros((ROPE_HALF,), F32)
    o32 = jnp.ones((ROPE_HALF,), F32)
    freq_tab = jnp.concatenate([inv_freq, z32, inv_freq, z32]).reshape(LANES, 1)
    sgn_tab = jnp.concatenate([-o32, z32, o32, z32]).reshape(LANES, 1)

    lane_pad = lambda v: jnp.pad(v.astype(F32), (0, LANES - n_ssd_heads)).reshape(1, LANES)
    a_neg = lane_pad(-jnp.exp(p["a_log"].astype(F32)) * LOG2_E)
    dt_bias = lane_pad(p["dt_bias"])
    d_skip_x = jnp.repeat(p["d_skip"].astype(F32), SSD_HEAD_DIM).reshape(1, d_inner)
    tri = jnp.tril(jnp.ones((T, T), BF16))
    expand = (jnp.arange(LANES)[:, None] == (jnp.arange(d_inner)[None, :] // SSD_HEAD_DIM)).astype(BF16)
    expand2 = jnp.concatenate([expand, expand], axis=0)

    tm = _tile(s, 512)
    assert tm % T == 0
    n_tiles = s // tm
    v_rows = V_HEAD_DIM + ONES_ROWS
    row_spec = lambda n: pl.BlockSpec((tm, n), lambda i: (i, 0))
    col_spec = lambda n: pl.BlockSpec((1, n, tm), lambda i: (i, 0, 0))
    q_t, kc, v_t, ssm = pl.pallas_call(
        functools.partial(_in_ssd_kernel, n_heads=n_mla_heads, q_scale=float(QK_HEAD_DIM) ** -0.5 * LOG2_E,
                          seq_tiles=n_tiles, d_inner=d_inner, n_ssd_heads=n_ssd_heads),
        grid=(m // tm,),
        in_specs=[row_spec(d_model), pl.BlockSpec((1, tm), lambda i: (0, i)), _resident((1, d_model)),
                  _resident((d_model, Q_LORA_RANK + KV_LORA_RANK)), _resident((d_model, d_inner)),
                  _resident((d_model, d_xbc)), _resident((d_model, 2 * LANES)),
                  _resident((1, Q_LORA_RANK)), _resident((n_mla_heads * QK_PAD, Q_LORA_RANK)),
                  _resident((1, KV_LORA_RANK)), _resident((KV_LORA_RANK, n_mla_heads * QK_NOPE_DIM)),
                  _resident((mla_width, KV_LORA_RANK)),
                  _resident((LANES, 1)), _resident((LANES, 1)),
                  _resident((SSD_CONV, d_xbc)), _resident((1, d_xbc)),
                  _resident((1, LANES)), _resident((1, LANES)), _resident((1, d_inner)), _resident((1, d_inner)),
                  _resident((T, T)), _resident((2 * LANES, d_inner))],
        out_specs=[col_spec(n_mla_heads * QK_PAD), row_spec(n_mla_heads * QK_PAD), col_spec(n_mla_heads * v_rows),
                   row_spec(d_inner)],
        out_shape=[jax.ShapeDtypeStruct((m // tm, n_mla_heads * QK_PAD, tm), BF16),
                   jax.ShapeDtypeStruct((m, n_mla_heads * QK_PAD), BF16),
                   jax.ShapeDtypeStruct((m // tm, n_mla_heads * v_rows, tm), BF16),
                   jax.ShapeDtypeStruct((m, d_inner), BF16)],
        scratch_shapes=[pltpu.VMEM((CONV_HALO + tm, d_xbc), F32),
                        pltpu.VMEM((tm // T, 3 * T, 2 * LANES), BF16),
                        pltpu.VMEM((SSD_GROUPS, SSD_STATE, d_inner // SSD_GROUPS), F32)],
        compiler_params=pltpu.CompilerParams(dimension_semantics=("arbitrary",),
                                             vmem_limit_bytes=VMEM_LIMIT_BYTES),
        name="in_ssd",
    )(h.reshape(m, d_model), pos_row, _row(p["pre_mix_norm_w"]), w_lat, w_z, w_xbc,
      w_misc, _row(p["q_norm_w"]), w_uq_t, _row(p["kv_norm_w"]), w_kn, w_v_t, freq_tab, sgn_tab,
      p["conv_w"].astype(F32), _row(p["conv_b"]), dt_bias, a_neg, d_skip_x, _row(p["ssd_norm_w"]),
      tri, expand2)

    hp = next(c for c in (4, 2, 1) if n_mla_heads % c == 0)
    attn_t = pl.pallas_call(
        functools.partial(_mla_kernel, tile=tm, heads=hp),
        grid=(b, n_mla_heads // hp, n_tiles),
        in_specs=[pl.BlockSpec((1, hp * QK_PAD, tm), lambda bi, hi, qi: (bi * n_tiles + qi, hi, 0)),
                  pl.BlockSpec((1, s, hp * QK_PAD), lambda bi, hi, qi: (bi, 0, hi)),
                  pl.BlockSpec((1, n_tiles, hp * v_rows, tm), lambda bi, hi, qi: (bi, 0, hi, 0))],
        out_specs=pl.BlockSpec((1, hp * V_HEAD_DIM, tm), lambda bi, hi, qi: (bi * n_tiles + qi, hi, 0)),
        out_shape=jax.ShapeDtypeStruct((m // tm, mla_width, tm), BF16),
        scratch_shapes=[pltpu.VMEM((hp, tm, tm), F32), pltpu.VMEM((hp, tm, tm), F32), pltpu.VMEM((hp, 1, tm), F32),
                        pltpu.VMEM((hp, v_rows, tm), F32)],
        compiler_params=pltpu.CompilerParams(dimension_semantics=("arbitrary", "arbitrary", "arbitrary"),
                                             vmem_limit_bytes=VMEM_LIMIT_BYTES),
        name="mla",
    )(q_t, kc.reshape(b, s, -1), v_t.reshape(b, n_tiles, n_mla_heads * v_rows, tm))

    w_out = p["w_out"].astype(BF16)
    orow = lambda n: pl.BlockSpec((tm, n), lambda i: (i, 0))
    h1 = pl.pallas_call(
        _out_proj_kernel,
        grid=(m // tm,),
        in_specs=[orow(d_model), col_spec(mla_width), orow(d_inner), _resident((mla_width, 1)),
                  _resident((mla_width, d_model)), _resident((d_inner, d_model)), _resident((1, d_model))],
        out_specs=orow(d_model),
        out_shape=jax.ShapeDtypeStruct((m, d_model), F32),
        compiler_params=pltpu.CompilerParams(dimension_semantics=("arbitrary",),
                                             vmem_limit_bytes=VMEM_LIMIT_BYTES),
        name="out_proj",
    )(h.reshape(m, d_model), attn_t, ssm, p["attn_out_norm_w"].astype(F32).reshape(mla_width, 1),
      w_out[:mla_width], w_out[mla_width:], _row(p["post_mix_norm_w"]))

    tf_m = _tile(m, 512)
    tf_f = _tile(d_ff, 512)
    assert d_ff // tf_f >= 2, "the FFN kernel's first and last d_ff steps must be distinct"
    out = pl.pallas_call(
        _ffn_kernel,
        grid=(m // tf_m, d_ff // tf_f),
        in_specs=[pl.BlockSpec((tf_m, d_model), lambda i, f: (i, 0)),
                  _resident((1, d_model)),
                  pl.BlockSpec((d_model, tf_f), lambda i, f: (0, f)),
                  pl.BlockSpec((d_model, tf_f), lambda i, f: (0, f)),
                  pl.BlockSpec((tf_f, d_model), lambda i, f: (f, 0)),
                  _resident((1, d_model))],
        out_specs=pl.BlockSpec((tf_m, d_model), lambda i, f: (i, 0)),
        out_shape=jax.ShapeDtypeStruct((m, d_model), F32),
        scratch_shapes=[pltpu.VMEM((tf_m, d_model), BF16)],
        compiler_params=pltpu.CompilerParams(dimension_semantics=("arbitrary", "arbitrary"),
                                             vmem_limit_bytes=VMEM_LIMIT_BYTES),
        name="ffn",
    )(h1, _row(p["pre_ffn_norm_w"]), p["w_gate"].astype(BF16), p["w_up"].astype(BF16),
      p["w_down"].astype(BF16), _row(p["post_ffn_norm_w"]))
    return out.reshape(b, s, d_model)


def kernel(x, positions, w_in, q_norm_w, w_uq, kv_norm_w, w_ukv, conv_w, conv_b, dt_bias, a_log, d_skip,
           ssd_norm_w, attn_out_norm_w, w_out, pre_mix_norm_w, post_mix_norm_w, pre_ffn_norm_w,
           post_ffn_norm_w, w_gate, w_up, w_down):
    stacked = dict(w_in=w_in, q_norm_w=q_norm_w, w_uq=w_uq, kv_norm_w=kv_norm_w, w_ukv=w_ukv, conv_w=conv_w,
                   conv_b=conv_b, dt_bias=dt_bias, a_log=a_log, d_skip=d_skip, ssd_norm_w=ssd_norm_w,
                   attn_out_norm_w=attn_out_norm_w, w_out=w_out, pre_mix_norm_w=pre_mix_norm_w,
                   post_mix_norm_w=post_mix_norm_w, pre_ffn_norm_w=pre_ffn_norm_w,
                   post_ffn_norm_w=post_ffn_norm_w, w_gate=w_gate, w_up=w_up, w_down=w_down)
    b, s, _ = x.shape
    pos_row = positions.astype(F32).reshape(1, b * s)
    h = x
    for l in range(w_in.shape[0]):
        h = _layer(h, pos_row, {k: v[l] for k, v in stacked.items()})
    return h
```

```python
import functools

import numpy as np
import jax
import jax.numpy as jnp
from jax import lax
from jax.experimental import pallas as pl
from jax.experimental.pallas import tpu as pltpu

F32 = jnp.float32
BF16 = jnp.bfloat16

V_HEAD_DIM = 128
QK_NOPE_DIM = 128
QK_ROPE_DIM = 64
QK_HEAD_DIM = QK_NOPE_DIM + QK_ROPE_DIM
Q_LORA_RANK = 512
KV_LORA_RANK = 512
ROPE_THETA = 10000.0
SSD_HEAD_DIM = 64
SSD_GROUPS = 2
SSD_STATE = 128
SSD_CONV = 4
SSD_CHUNK = 128
EPS = 1e-6

LANES = 128
SUBLANES = 8
VMEM_LIMIT_BYTES = 56 * 1024 * 1024

QK_PAD = 2 * LANES
ROPE_HALF = QK_ROPE_DIM // 2
LOG2_E = 1.4426950408889634
ONES_ROWS = 2 * SUBLANES
CONV_HALO = SUBLANES


def _rms(t, w):
    return t * lax.rsqrt(jnp.mean(t * t, axis=-1, keepdims=True) + EPS) * w


def _silu(t):
    h = 0.5 * t
    return h + h * jnp.tanh(h)


def _rope(t, cos, sin_signed):
    return t * cos + pltpu.roll(t, LANES // 2, axis=1) * sin_signed


def _ssd_chunk(k, ext_ref, hilo_ref, state_ref, z_c, dt_c, convw_ref, convb_ref, dtb_ref, aneg_ref,
               dskip_ref, nw_ref, tri_ref, e_ref, *, d_inner, n_heads):
    T, N, P, G = SSD_CHUNK, SSD_STATE, SSD_HEAD_DIM, SSD_GROUPS
    gw = d_inner // G

    window = ext_ref[k * T:k * T + CONV_HALO + T, :]
    conv = convb_ref[...] + convw_ref[SSD_CONV - 1:SSD_CONV, :] * window[CONV_HALO:]
    for tap in range(SSD_CONV - 1):
        shifted = pltpu.roll(window, SSD_CONV - 1 - tap, axis=0)[CONV_HALO:]
        conv = conv + convw_ref[tap:tap + 1, :] * shifted
    xbc = _silu(conv)
    xs = xbc[:, :d_inner]
    bm = xbc[:, d_inner:d_inner + G * N]
    cm = xbc[:, d_inner + G * N:]

    dt_in = dt_c + dtb_ref[...]
    dt = jnp.maximum(dt_in, 0.0) + jnp.log1p(jnp.exp(-jnp.abs(dt_in)))
    a = dt * aneg_ref[...]
    a_hi = a.astype(BF16)
    a_r = a - a_hi.astype(F32)
    a_mid = a_r.astype(BF16)
    a_lo = (a_r - a_mid.astype(F32)).astype(BF16)
    tri_b = tri_ref[...]
    a_cum = (jnp.dot(tri_b, a_hi, preferred_element_type=F32) + jnp.dot(tri_b, a_mid, preferred_element_type=F32)
             + jnp.dot(tri_b, a_lo, preferred_element_type=F32))
    a_cum_t = a_cum.T
    a_last = a_cum[T - 1:T, :]
    ea = jnp.exp2(a_cum)
    ds = jnp.exp2(a_last - a_cum)

    stacked = jnp.concatenate([dt, ea, ds], axis=0)
    s_hi = stacked.astype(BF16)
    hilo_ref[k, :, :LANES] = s_hi
    hilo_ref[k, :, LANES:] = (stacked - s_hi.astype(F32)).astype(BF16)
    expanded = jnp.dot(hilo_ref[k], e_ref[...], preferred_element_type=F32)
    dt_x, ea_x, ds_x = expanded[0:T], expanded[T:2 * T], expanded[2 * T:3 * T]

    xdt = xs * dt_x
    xdt_b = xdt.astype(BF16)
    xw_b = (xdt * ds_x).astype(BF16)

    row = lax.broadcasted_iota(jnp.int32, (T, T), 0)
    col = lax.broadcasted_iota(jnp.int32, (T, T), 1)
    tri = col <= row
    lane = lax.broadcasted_iota(jnp.int32, (T, 2 * P), 1)
    heads_per_group = n_heads // G

    y_parts = []
    y_off_parts = []
    for g in range(G):
        cols = slice(g * gw, (g + 1) * gw)
        bm_g = bm[:, g * N:(g + 1) * N]
        cm_g = cm[:, g * N:(g + 1) * N].astype(BF16)
        cb = lax.dot_general(cm_g, bm_g.astype(BF16), (((1,), (1,)), ((), ())), preferred_element_type=F32)
        for j in range(heads_per_group // 2):
            c0 = g * gw + 2 * j * P
            xp = xdt_b[:, c0:c0 + 2 * P]
            zero = jnp.zeros_like(xp)
            part = None
            for i, x_half in enumerate((jnp.where(lane < P, xp, zero), jnp.where(lane >= P, xp, zero))):
                h = g * heads_per_group + 2 * j + i
                seg = a_cum[:, h:h + 1] - a_cum_t[h:h + 1, :]
                w = (cb * jnp.exp2(jnp.where(tri, seg, -jnp.inf))).astype(BF16)
                d = jnp.dot(w, x_half, preferred_element_type=F32)
                part = d if part is None else part + d
            y_parts.append(part)
        prev = state_ref[g]
        y_off_parts.append(jnp.dot(cm_g, prev.astype(BF16), preferred_element_type=F32))
        new = jnp.dot(bm_g.T.astype(BF16), xw_b[:, cols], preferred_element_type=F32)
        state_ref[g] = prev * ea_x[T - 1:T, cols] + new
    y_diag = jnp.concatenate(y_parts, axis=1)
    y_off = jnp.concatenate(y_off_parts, axis=1) * ea_x

    y = y_diag + y_off + xs * dskip_ref[...]
    gated = y * _silu(z_c)
    outs = []
    for g in range(G):
        gg = gated[:, g * gw:(g + 1) * gw]
        outs.append(gg * lax.rsqrt(jnp.mean(gg * gg, axis=-1, keepdims=True) + EPS))
    return (jnp.concatenate(outs, axis=1) * nw_ref[...]).astype(BF16)


def _in_ssd_kernel(x_ref, pos_ref, wpre_ref, wlat_ref, wz_ref, wxbc_ref, wmisc_ref, qnw_ref, wuqt_ref, kvnw_ref,
                   wkn_ref, wvt_ref, freq_ref, sgn_ref, convw_ref, convb_ref, dtb_ref, aneg_ref, dskip_ref,
                   nw_ref, tri_ref, e_ref, qt_ref, kc_ref, vt_ref, ssm_ref, ext_ref, hilo_ref, state_ref,
                   *, n_heads, q_scale, seq_tiles, d_inner, n_ssd_heads):
    T = SSD_CHUNK
    tm = x_ref.shape[0]

    @pl.when(pl.program_id(0) % seq_tiles == 0)
    def _():
        ext_ref[0:CONV_HALO, :] = jnp.zeros((CONV_HALO, ext_ref.shape[1]), F32)
        state_ref[...] = jnp.zeros_like(state_ref)

    u = _rms(x_ref[...], wpre_ref[...]).astype(BF16)

    nt_dims = (((1,), (1,)), ((), ()))
    ext_ref[CONV_HALO:, :] = lax.dot_general(u, wxbc_ref[...], nt_dims, preferred_element_type=F32)
    z = lax.dot_general(u, wz_ref[...], nt_dims, preferred_element_type=F32)
    misc = lax.dot_general(u, wmisc_ref[...], nt_dims, preferred_element_type=F32)
    dt_raw = misc[:, LANES:]
    for k in range(tm // T):
        ssm_ref[k * T:(k + 1) * T, :] = _ssd_chunk(
            k, ext_ref, hilo_ref, state_ref, z[k * T:(k + 1) * T], dt_raw[k * T:(k + 1) * T],
            convw_ref, convb_ref, dtb_ref, aneg_ref, dskip_ref, nw_ref, tri_ref, e_ref,
            d_inner=d_inner, n_heads=n_ssd_heads)
    ext_ref[0:CONV_HALO, :] = ext_ref[tm:tm + CONV_HALO, :]

    ang_t = freq_ref[...] * pos_ref[...]
    cos_t = jnp.cos(ang_t)
    sin_t = jnp.sin(ang_t) * sgn_ref[...]
    k_rope = _rope(misc[:, :LANES], cos_t.T, sin_t.T).astype(BF16)

    cqkv = lax.dot_general(u, wlat_ref[...], nt_dims, preferred_element_type=F32)
    cqn = _rms(cqkv[:, :Q_LORA_RANK], qnw_ref[...]).astype(BF16)
    ckvn = _rms(cqkv[:, Q_LORA_RANK:], kvnw_ref[...]).astype(BF16)
    q_t = lax.dot_general(wuqt_ref[...], cqn, nt_dims, preferred_element_type=F32)
    cos_q = cos_t * q_scale
    sin_q = sin_t * q_scale
    half = LANES // 2
    for h in range(n_heads):
        base = h * QK_PAD
        qt_ref[0, base:base + LANES, :] = (q_t[base:base + LANES] * q_scale).astype(BF16)
        blk = q_t[base + LANES:base + QK_PAD]
        swapped = jnp.concatenate([blk[half:], blk[:half]], axis=0)
        qt_ref[0, base + LANES:base + QK_PAD, :] = (blk * cos_q + swapped * sin_q).astype(BF16)

    k_nope = jnp.dot(ckvn, wkn_ref[...], preferred_element_type=F32).astype(BF16)
    v_t = lax.dot_general(wvt_ref[...], ckvn, nt_dims, preferred_element_type=F32)
    v_rows = V_HEAD_DIM + ONES_ROWS
    for h in range(n_heads):
        kc_ref[:, h * QK_PAD:h * QK_PAD + LANES] = k_nope[:, h * LANES:(h + 1) * LANES]
        kc_ref[:, h * QK_PAD + LANES:(h + 1) * QK_PAD] = k_rope
        vt_ref[0, h * v_rows:h * v_rows + V_HEAD_DIM, :] = v_t[h * V_HEAD_DIM:(h + 1) * V_HEAD_DIM].astype(BF16)
        vt_ref[0, h * v_rows + V_HEAD_DIM:(h + 1) * v_rows, :] = jnp.ones((ONES_ROWS, tm), BF16)


def _mla_kernel(qt_ref, kc_ref, vt_ref, o_ref, sa_ref, sb_ref, m_ref, acc_ref, *, tile, heads):
    qi = pl.program_id(2)
    v_rows = V_HEAD_DIM + ONES_ROWS

    m_ref[...] = jnp.full_like(m_ref, -jnp.inf)
    acc_ref[...] = jnp.zeros_like(acc_ref)

    def scores(j, s_ref):
        rows = pl.ds(pl.multiple_of(j * tile, tile), tile)
        for h in range(heads):
            s_ref[h] = jnp.dot(kc_ref[0, rows, h * QK_PAD:(h + 1) * QK_PAD], qt_ref[0, h * QK_PAD:(h + 1) * QK_PAD, :],
                               preferred_element_type=F32)

    def consume(j, s_ref, masked):
        for h in range(heads):
            s = s_ref[h]
            if masked:
                k_idx = lax.broadcasted_iota(jnp.int32, s.shape, 0)
                q_idx = lax.broadcasted_iota(jnp.int32, s.shape, 1)
                s = jnp.where(k_idx <= q_idx, s, -jnp.inf)
            m_prev = m_ref[h]
            m_new = jnp.maximum(m_prev, jnp.max(s, axis=0, keepdims=True))
            alpha = jnp.exp2(m_prev - m_new)
            p = jnp.exp2(s - m_new)
            acc_ref[h] = alpha * acc_ref[h] + jnp.dot(vt_ref[0, j, h * v_rows:(h + 1) * v_rows, :], p.astype(BF16),
                                                      preferred_element_type=F32)
            m_ref[h] = m_new

    def pair(jj, carry):
        j = 2 * jj
        scores(j + 1, sb_ref)
        consume(j, sa_ref, False)
        scores(j + 2, sa_ref)
        consume(j + 1, sb_ref, False)
        return carry

    scores(0, sa_ref)
    lax.fori_loop(0, qi // 2, pair, 0)

    @pl.when(qi % 2 == 0)
    def _():
        consume(qi, sa_ref, True)

    @pl.when(qi % 2 == 1)
    def _():
        scores(qi, sb_ref)
        consume(qi - 1, sa_ref, False)
        consume(qi, sb_ref, True)

    for h in range(heads):
        out = acc_ref[h, :V_HEAD_DIM] / acc_ref[h, V_HEAD_DIM:V_HEAD_DIM + 1]
        o_ref[0, h * V_HEAD_DIM:(h + 1) * V_HEAD_DIM, :] = out.astype(o_ref.dtype)


def _out_proj_kernel(x_ref, attn_ref, ssm_ref, anw_ref, woa_ref, wos_ref, postw_ref, h_ref):
    a_t = attn_ref[0].astype(F32)
    scale = lax.rsqrt(jnp.mean(a_t * a_t, axis=0, keepdims=True) + EPS)
    attn_n = (a_t * scale * anw_ref[...]).astype(BF16).T
    mix = jnp.dot(attn_n, woa_ref[...], preferred_element_type=F32)
    mix = mix + jnp.dot(ssm_ref[...], wos_ref[...], preferred_element_type=F32)
    h_ref[...] = x_ref[...] + _rms(mix, postw_ref[...])


def _ffn_kernel(h_ref, prew_ref, wg_ref, wu_ref, wd_ref, postw_ref, o_ref, v_ref):
    f = pl.program_id(1)
    last = pl.num_programs(1) - 1
    tm = h_ref.shape[0]
    row_halves = (slice(0, tm // 2), slice(tm // 2, tm))

    def prenorm(rows):
        v_ref[rows, :] = _rms(h_ref[rows, :], prew_ref[...]).astype(BF16)
        o_ref[rows, :] = jnp.zeros((tm // 2, o_ref.shape[1]), F32)

    def swiglu(rows):
        v = v_ref[rows, :]
        half = wg_ref.shape[1] // 2
        acts = []
        for c in range(2):
            cols = slice(c * half, (c + 1) * half)
            gate = jnp.dot(v, wg_ref[:, cols], preferred_element_type=F32)
            up = jnp.dot(v, wu_ref[:, cols], preferred_element_type=F32)
            acts.append((_silu(gate) * up).astype(BF16))
        o_ref[rows, :] += jnp.dot(jnp.concatenate(acts, axis=1), wd_ref[...], preferred_element_type=F32)

    def finalize(rows):
        o_ref[rows, :] = h_ref[rows, :] + _rms(o_ref[rows, :], postw_ref[...])

    @pl.when(f == 0)
    def _():
        prenorm(row_halves[0])
        swiglu(row_halves[0])
        prenorm(row_halves[1])
        swiglu(row_halves[1])

    @pl.when((f > 0) & (f < last))
    def _():
        swiglu(slice(0, tm))

    @pl.when(f == last)
    def _():
        swiglu(row_halves[0])
        finalize(row_halves[0])
        swiglu(row_halves[1])
        finalize(row_halves[1])


def _tile(n, want):
    t = min(n, want)
    assert n % t == 0, (n, t)
    return t


def _resident(shape):
    return pl.BlockSpec(shape, lambda *_: (0,) * len(shape), pipeline_mode=pl.Buffered(1))


def _row(w):
    return w.reshape(1, -1).astype(F32)


def _layer(h, pos_row, p):
    b, s, d_model = h.shape
    m = b * s
    n_mla_heads = p["w_uq"].shape[1] // QK_HEAD_DIM
    mla_width = n_mla_heads * V_HEAD_DIM
    d_inner = p["ssd_norm_w"].shape[0]
    n_ssd_heads = p["dt_bias"].shape[0]
    d_xbc = d_inner + 2 * SSD_GROUPS * SSD_STATE
    d_ff = p["w_gate"].shape[1]
    T = SSD_CHUNK
    assert n_ssd_heads * SSD_HEAD_DIM == d_inner and n_ssd_heads <= LANES

    w_in = p["w_in"].T.astype(BF16)
    offs = np.cumsum([0, Q_LORA_RANK, KV_LORA_RANK, QK_ROPE_DIM, d_inner, d_xbc, n_ssd_heads])
    w_cq, w_ckv, w_kr, w_z, w_xbc, w_dt = [w_in[offs[i]:offs[i + 1]] for i in range(6)]
    zpad = lambda n: jnp.zeros((n, d_model), BF16)
    w_lat = w_in[:offs[2]]
    w_misc = jnp.concatenate([w_kr[:ROPE_HALF], zpad(ROPE_HALF), w_kr[ROPE_HALF:], zpad(ROPE_HALF),
                              w_dt, zpad(LANES - n_ssd_heads)], axis=0)

    wq3 = p["w_uq"].reshape(Q_LORA_RANK, n_mla_heads, QK_HEAD_DIM)
    zq = jnp.zeros((Q_LORA_RANK, n_mla_heads, ROPE_HALF), wq3.dtype)
    w_uq_r = jnp.concatenate([
        wq3[..., :QK_NOPE_DIM], wq3[..., QK_NOPE_DIM:QK_NOPE_DIM + ROPE_HALF], zq,
        wq3[..., QK_NOPE_DIM + ROPE_HALF:], zq], axis=-1).reshape(Q_LORA_RANK, n_mla_heads * QK_PAD).astype(BF16)
    w_uq_t = w_uq_r.T
    wkv3 = p["w_ukv"].astype(BF16).reshape(KV_LORA_RANK, n_mla_heads, QK_NOPE_DIM + V_HEAD_DIM)
    w_kn = wkv3[..., :QK_NOPE_DIM].reshape(KV_LORA_RANK, n_mla_heads * QK_NOPE_DIM)
    w_v_t = wkv3[..., QK_NOPE_DIM:].reshape(KV_LORA_RANK, mla_width).T

    inv_freq = ROPE_THETA ** (-jnp.arange(0, QK_ROPE_DIM, 2, dtype=F32) / QK_ROPE_DIM)
    z32 = jnp.zeros((ROPE_HALF,), F32)
    o32 = jnp.ones((ROPE_HALF,), F32)
    freq_tab = jnp.concatenate([inv_freq, z32, inv_freq, z32]).reshape(LANES, 1)
    sgn_tab = jnp.concatenate([-o32, z32, o32, z32]).reshape(LANES, 1)

    lane_pad = lambda v: jnp.pad(v.astype(F32), (0, LANES - n_ssd_heads)).reshape(1, LANES)
    a_neg = lane_pad(-jnp.exp(p["a_log"].astype(F32)) * LOG2_E)
    dt_bias = lane_pad(p["dt_bias"])
    d_skip_x = jnp.repeat(p["d_skip"].astype(F32), SSD_HEAD_DIM).reshape(1, d_inner)
    tri = jnp.tril(jnp.ones((T, T), BF16))
    expand = (jnp.arange(LANES)[:, None] == (jnp.arange(d_inner)[None, :] // SSD_HEAD_DIM)).astype(BF16)
    expand2 = jnp.concatenate([expand, expand], axis=0)

    tm = _tile(s, 512)
    assert tm % T == 0
    n_tiles = s // tm
    v_rows = V_HEAD_DIM + ONES_ROWS
    row_spec = lambda n: pl.BlockSpec((tm, n), lambda i: (i, 0))
    col_spec = lambda n: pl.BlockSpec((1, n, tm), lambda i: (i, 0, 0))
    q_t, kc, v_t, ssm = pl.pallas_call(
        functools.partial(_in_ssd_kernel, n_heads=n_mla_heads, q_scale=float(QK_HEAD_DIM) ** -0.5 * LOG2_E,
                          seq_tiles=n_tiles, d_inner=d_inner, n_ssd_heads=n_ssd_heads),
        grid=(m // tm,),
        in_specs=[row_spec(d_model), pl.BlockSpec((1, tm), lambda i: (0, i)), _resident((1, d_model)),
                  _resident((Q_LORA_RANK + KV_LORA_RANK, d_model)), _resident((d_inner, d_model)),
                  _resident((d_xbc, d_model)), _resident((2 * LANES, d_model)),
                  _resident((1, Q_LORA_RANK)), _resident((n_mla_heads * QK_PAD, Q_LORA_RANK)),
                  _resident((1, KV_LORA_RANK)), _resident((KV_LORA_RANK, n_mla_heads * QK_NOPE_DIM)),
                  _resident((mla_width, KV_LORA_RANK)),
                  _resident((LANES, 1)), _resident((LANES, 1)),
                  _resident((SSD_CONV, d_xbc)), _resident((1, d_xbc)),
                  _resident((1, LANES)), _resident((1, LANES)), _resident((1, d_inner)), _resident((1, d_inner)),
                  _resident((T, T)), _resident((2 * LANES, d_inner))],
        out_specs=[col_spec(n_mla_heads * QK_PAD), row_spec(n_mla_heads * QK_PAD), col_spec(n_mla_heads * v_rows),
                   row_spec(d_inner)],
        out_shape=[jax.ShapeDtypeStruct((m // tm, n_mla_heads * QK_PAD, tm), BF16),
                   jax.ShapeDtypeStruct((m, n_mla_heads * QK_PAD), BF16),
                   jax.ShapeDtypeStruct((m // tm, n_mla_heads * v_rows, tm), BF16),
                   jax.ShapeDtypeStruct((m, d_inner), BF16)],
        scratch_shapes=[pltpu.VMEM((CONV_HALO + tm, d_xbc), F32),
                        pltpu.VMEM((tm // T, 3 * T, 2 * LANES), BF16),
                        pltpu.VMEM((SSD_GROUPS, SSD_STATE, d_inner // SSD_GROUPS), F32)],
        compiler_params=pltpu.CompilerParams(dimension_semantics=("arbitrary",),
                                             vmem_limit_bytes=VMEM_LIMIT_BYTES),
        name="in_ssd",
    )(h.reshape(m, d_model), pos_row, _row(p["pre_mix_norm_w"]), w_lat, w_z, w_xbc,
      w_misc, _row(p["q_norm_w"]), w_uq_t, _row(p["kv_norm_w"]), w_kn, w_v_t, freq_tab, sgn_tab,
      p["conv_w"].astype(F32), _row(p["conv_b"]), dt_bias, a_neg, d_skip_x, _row(p["ssd_norm_w"]),
      tri, expand2)

    hp = next(c for c in (4, 2, 1) if n_mla_heads % c == 0)
    attn_t = pl.pallas_call(
        functools.partial(_mla_kernel, tile=tm, heads=hp),
        grid=(b, n_mla_heads // hp, n_tiles),
        in_specs=[pl.BlockSpec((1, hp * QK_PAD, tm), lambda bi, hi, qi: (bi * n_tiles + qi, hi, 0)),
                  pl.BlockSpec((1, s, hp * QK_PAD), lambda bi, hi, qi: (bi, 0, hi)),
                  pl.BlockSpec((1, n_tiles, hp * v_rows, tm), lambda bi, hi, qi: (bi, 0, hi, 0))],
        out_specs=pl.BlockSpec((1, hp * V_HEAD_DIM, tm), lambda bi, hi, qi: (bi * n_tiles + qi, hi, 0)),
        out_shape=jax.ShapeDtypeStruct((m // tm, mla_width, tm), BF16),
        scratch_shapes=[pltpu.VMEM((hp, tm, tm), F32), pltpu.VMEM((hp, tm, tm), F32), pltpu.VMEM((hp, 1, tm), F32),
                        pltpu.VMEM((hp, v_rows, tm), F32)],
        compiler_params=pltpu.CompilerParams(dimension_semantics=("arbitrary", "arbitrary", "arbitrary"),
                                             vmem_limit_bytes=VMEM_LIMIT_BYTES),
        name="mla",
    )(q_t, kc.reshape(b, s, -1), v_t.reshape(b, n_tiles, n_mla_heads * v_rows, tm))

    w_out = p["w_out"].astype(BF16)
    orow = lambda n: pl.BlockSpec((tm, n), lambda i: (i, 0))
    h1 = pl.pallas_call(
        _out_proj_kernel,
        grid=(m // tm,),
        in_specs=[orow(d_model), col_spec(mla_width), orow(d_inner), _resident((mla_width, 1)),
                  _resident((mla_width, d_model)), _resident((d_inner, d_model)), _resident((1, d_model))],
        out_specs=orow(d_model),
        out_shape=jax.ShapeDtypeStruct((m, d_model), F32),
        compiler_params=pltpu.CompilerParams(dimension_semantics=("arbitrary",),
                                             vmem_limit_bytes=VMEM_LIMIT_BYTES),
        name="out_proj",
    )(h.reshape(m, d_model), attn_t, ssm, p["attn_out_norm_w"].astype(F32).reshape(mla_width, 1),
      w_out[:mla_width], w_out[mla_width:], _row(p["post_mix_norm_w"]))

    tf_m = _tile(m, 512)
    tf_f = _tile(d_ff, 512)
    assert d_ff // tf_f >= 2, "the FFN kernel's first and last d_ff steps must be distinct"
    out = pl.pallas_call(
        _ffn_kernel,
        grid=(m // tf_m, d_ff // tf_f),
        in_specs=[pl.BlockSpec((tf_m, d_model), lambda i, f: (i, 0)),
                  _resident((1, d_model)),
                  pl.BlockSpec((d_model, tf_f), lambda i, f: (0, f)),
                  pl.BlockSpec((d_model, tf_f), lambda i, f: (0, f)),
                  pl.BlockSpec((tf_f, d_model), lambda i, f: (f, 0)),
                  _resident((1, d_model))],
        out_specs=pl.BlockSpec((tf_m, d_model), lambda i, f: (i, 0)),
        out_shape=jax.ShapeDtypeStruct((m, d_model), F32),
        scratch_shapes=[pltpu.VMEM((tf_m, d_model), BF16)],
        compiler_params=pltpu.CompilerParams(dimension_semantics=("arbitrary", "arbitrary"),
                                             vmem_limit_bytes=VMEM_LIMIT_BYTES),
        name="ffn",
    )(h1, _row(p["pre_ffn_norm_w"]), p["w_gate"].astype(BF16), p["w_up"].astype(BF16),
      p["w_down"].astype(BF16), _row(p["post_ffn_norm_w"]))
    return out.reshape(b, s, d_model)


def kernel(x, positions, w_in, q_norm_w, w_uq, kv_norm_w, w_ukv, conv_w, conv_b, dt_bias, a_log, d_skip,
           ssd_norm_w, attn_out_norm_w, w_out, pre_mix_norm_w, post_mix_norm_w, pre_ffn_norm_w,
           post_ffn_norm_w, w_gate, w_up, w_down):
    stacked = dict(w_in=w_in, q_norm_w=q_norm_w, w_uq=w_uq, kv_norm_w=kv_norm_w, w_ukv=w_ukv, conv_w=conv_w,
                   conv_b=conv_b, dt_bias=dt_bias, a_log=a_log, d_skip=d_skip, ssd_norm_w=ssd_norm_w,
                   attn_out_norm_w=attn_out_norm_w, w_out=w_out, pre_mix_norm_w=pre_mix_norm_w,
                   post_mix_norm_w=post_mix_norm_w, pre_ffn_norm_w=pre_ffn_norm_w,
                   post_ffn_norm_w=post_ffn_norm_w, w_gate=w_gate, w_up=w_up, w_down=w_down)
    b, s, _ = x.shape
    pos_row = positions.astype(F32).reshape(1, b * s)
    h = x
    for l in range(w_in.shape[0]):
        h = _layer(h, pos_row, {k: v[l] for k, v in stacked.items()})
    return h
```

```python
import functools

import numpy as np
import jax
import jax.numpy as jnp
from jax import lax
from jax.experimental import pallas as pl
from jax.experimental.pallas import tpu as pltpu

F32 = jnp.float32
BF16 = jnp.bfloat16

V_HEAD_DIM = 128
QK_NOPE_DIM = 128
QK_ROPE_DIM = 64
QK_HEAD_DIM = QK_NOPE_DIM + QK_ROPE_DIM
Q_LORA_RANK = 512
KV_LORA_RANK = 512
ROPE_THETA = 10000.0
SSD_HEAD_DIM = 64
SSD_GROUPS = 2
SSD_STATE = 128
SSD_CONV = 4
SSD_CHUNK = 128
EPS = 1e-6

LANES = 128
SUBLANES = 8
VMEM_LIMIT_BYTES = 56 * 1024 * 1024

QK_PAD = 2 * LANES
ROPE_HALF = QK_ROPE_DIM // 2
LOG2_E = 1.4426950408889634
ONES_ROWS = 2 * SUBLANES
CONV_HALO = SUBLANES


def _rms(t, w):
    return t * lax.rsqrt(jnp.mean(t * t, axis=-1, keepdims=True) + EPS) * w


def _silu(t):
    h = 0.5 * t
    return h + h * jnp.tanh(h)


def _rope(t, cos, sin_signed):
    return t * cos + pltpu.roll(t, LANES // 2, axis=1) * sin_signed


def _ssd_chunk(k, ext_ref, hilo_ref, state_ref, z_c, dt_c, convw_ref, convb_ref, dtb_ref, aneg_ref,
               dskip_ref, nw_ref, tri_ref, e_ref, *, d_inner, n_heads):
    T, N, P, G = SSD_CHUNK, SSD_STATE, SSD_HEAD_DIM, SSD_GROUPS
    gw = d_inner // G

    window = ext_ref[k * T:k * T + CONV_HALO + T, :]
    conv = convb_ref[...] + convw_ref[SSD_CONV - 1:SSD_CONV, :] * window[CONV_HALO:]
    for tap in range(SSD_CONV - 1):
        shifted = pltpu.roll(window, SSD_CONV - 1 - tap, axis=0)[CONV_HALO:]
        conv = conv + convw_ref[tap:tap + 1, :] * shifted
    xbc = _silu(conv)
    xs = xbc[:, :d_inner]
    bm = xbc[:, d_inner:d_inner + G * N]
    cm = xbc[:, d_inner + G * N:]

    dt_in = dt_c + dtb_ref[...]
    dt = jnp.maximum(dt_in, 0.0) + jnp.log1p(jnp.exp(-jnp.abs(dt_in)))
    a = dt * aneg_ref[...]
    a_hi = a.astype(BF16)
    a_r = a - a_hi.astype(F32)
    a_mid = a_r.astype(BF16)
    a_lo = (a_r - a_mid.astype(F32)).astype(BF16)
    tri_b = tri_ref[...]
    a_cum = (jnp.dot(tri_b, a_hi, preferred_element_type=F32) + jnp.dot(tri_b, a_mid, preferred_element_type=F32)
             + jnp.dot(tri_b, a_lo, preferred_element_type=F32))
    a_cum_t = a_cum.T
    a_last = a_cum[T - 1:T, :]
    ea = jnp.exp2(a_cum)
    ds = jnp.exp2(a_last - a_cum)

    stacked = jnp.concatenate([dt, ea, ds], axis=0)
    s_hi = stacked.astype(BF16)
    hilo_ref[k, :, :LANES] = s_hi
    hilo_ref[k, :, LANES:] = (stacked - s_hi.astype(F32)).astype(BF16)
    expanded = jnp.dot(hilo_ref[k], e_ref[...], preferred_element_type=F32)
    dt_x, ea_x, ds_x = expanded[0:T], expanded[T:2 * T], expanded[2 * T:3 * T]

    xdt = xs * dt_x
    xdt_b = xdt.astype(BF16)
    xw_b = (xdt * ds_x).astype(BF16)

    row = lax.broadcasted_iota(jnp.int32, (T, T), 0)
    col = lax.broadcasted_iota(jnp.int32, (T, T), 1)
    tri = col <= row
    lane = lax.broadcasted_iota(jnp.int32, (T, 2 * P), 1)
    heads_per_group = n_heads // G

    y_parts = []
    y_off_parts = []
    for g in range(G):
        cols = slice(g * gw, (g + 1) * gw)
        bm_g = bm[:, g * N:(g + 1) * N]
        cm_g = cm[:, g * N:(g + 1) * N].astype(BF16)
        cb = lax.dot_general(cm_g, bm_g.astype(BF16), (((1,), (1,)), ((), ())), preferred_element_type=F32)
        for j in range(heads_per_group // 2):
            c0 = g * gw + 2 * j * P
            xp = xdt_b[:, c0:c0 + 2 * P]
            zero = jnp.zeros_like(xp)
            part = None
            for i, x_half in enumerate((jnp.where(lane < P, xp, zero), jnp.where(lane >= P, xp, zero))):
                h = g * heads_per_group + 2 * j + i
                seg = a_cum[:, h:h + 1] - a_cum_t[h:h + 1, :]
                w = (cb * jnp.exp2(jnp.where(tri, seg, -jnp.inf))).astype(BF16)
                d = jnp.dot(w, x_half, preferred_element_type=F32)
                part = d if part is None else part + d
            y_parts.append(part)
        prev = state_ref[g]
        y_off_parts.append(jnp.dot(cm_g, prev.astype(BF16), preferred_element_type=F32))
        new = jnp.dot(bm_g.T.astype(BF16), xw_b[:, cols], preferred_element_type=F32)
        state_ref[g] = prev * ea_x[T - 1:T, cols] + new
    y_diag = jnp.concatenate(y_parts, axis=1)
    y_off = jnp.concatenate(y_off_parts, axis=1) * ea_x

    y = y_diag + y_off + xs * dskip_ref[...]
    gated = y * _silu(z_c)
    outs = []
    for g in range(G):
        gg = gated[:, g * gw:(g + 1) * gw]
        outs.append(gg * lax.rsqrt(jnp.mean(gg * gg, axis=-1, keepdims=True) + EPS))
    return (jnp.concatenate(outs, axis=1) * nw_ref[...]).astype(BF16)


def _in_ssd_kernel(x_ref, pos_ref, wpre_ref, wlat_ref, wz_ref, wxbc_ref, wmisc_ref, qnw_ref, wuqt_ref, kvnw_ref,
                   wkn_ref, wvt_ref, freq_ref, sgn_ref, convw_ref, convb_ref, dtb_ref, aneg_ref, dskip_ref,
                   nw_ref, tri_ref, e_ref, qt_ref, kc_ref, vt_ref, ssm_ref, ext_ref, hilo_ref, state_ref,
                   *, n_heads, q_scale, seq_tiles, d_inner, n_ssd_heads):
    T = SSD_CHUNK
    tm = x_ref.shape[0]

    @pl.when(pl.program_id(0) % seq_tiles == 0)
    def _():
        ext_ref[0:CONV_HALO, :] = jnp.zeros((CONV_HALO, ext_ref.shape[1]), F32)
        state_ref[...] = jnp.zeros_like(state_ref)

    u = _rms(x_ref[...], wpre_ref[...]).astype(BF16)

    nt_dims = (((1,), (1,)), ((), ()))
    ext_ref[CONV_HALO:, :] = lax.dot_general(u, wxbc_ref[...], nt_dims, preferred_element_type=F32)
    z = lax.dot_general(u, wz_ref[...], nt_dims, preferred_element_type=F32)
    misc = lax.dot_general(u, wmisc_ref[...], nt_dims, preferred_element_type=F32)
    dt_raw = misc[:, LANES:]
    for k in range(tm // T):
        ssm_ref[k * T:(k + 1) * T, :] = _ssd_chunk(
            k, ext_ref, hilo_ref, state_ref, z[k * T:(k + 1) * T], dt_raw[k * T:(k + 1) * T],
            convw_ref, convb_ref, dtb_ref, aneg_ref, dskip_ref, nw_ref, tri_ref, e_ref,
            d_inner=d_inner, n_heads=n_ssd_heads)
    ext_ref[0:CONV_HALO, :] = ext_ref[tm:tm + CONV_HALO, :]

    ang_t = freq_ref[...] * pos_ref[...]
    cos_t = jnp.cos(ang_t)
    sin_t = jnp.sin(ang_t) * sgn_ref[...]
    k_rope = _rope(misc[:, :LANES], cos_t.T, sin_t.T).astype(BF16)

    cqkv = lax.dot_general(u, wlat_ref[...], nt_dims, preferred_element_type=F32)
    cqn = _rms(cqkv[:, :Q_LORA_RANK], qnw_ref[...]).astype(BF16)
    ckvn = _rms(cqkv[:, Q_LORA_RANK:], kvnw_ref[...]).astype(BF16)
    q_t = lax.dot_general(wuqt_ref[...], cqn, nt_dims, preferred_element_type=F32)
    cos_q = cos_t * q_scale
    sin_q = sin_t * q_scale
    half = LANES // 2
    for h in range(n_heads):
        base = h * QK_PAD
        qt_ref[0, base:base + LANES, :] = (q_t[base:base + LANES] * q_scale).astype(BF16)
        blk = q_t[base + LANES:base + QK_PAD]
        swapped = jnp.concatenate([blk[half:], blk[:half]], axis=0)
        qt_ref[0, base + LANES:base + QK_PAD, :] = (blk * cos_q + swapped * sin_q).astype(BF16)

    k_nope = jnp.dot(ckvn, wkn_ref[...], preferred_element_type=F32).astype(BF16)
    v_t = lax.dot_general(wvt_ref[...], ckvn, nt_dims, preferred_element_type=F32)
    v_rows = V_HEAD_DIM + ONES_ROWS
    for h in range(n_heads):
        kc_ref[:, h * QK_PAD:h * QK_PAD + LANES] = k_nope[:, h * LANES:(h + 1) * LANES]
        kc_ref[:, h * QK_PAD + LANES:(h + 1) * QK_PAD] = k_rope
        vt_ref[0, h * v_rows:h * v_rows + V_HEAD_DIM, :] = v_t[h * V_HEAD_DIM:(h + 1) * V_HEAD_DIM].astype(BF16)
        vt_ref[0, h * v_rows + V_HEAD_DIM:(h + 1) * v_rows, :] = jnp.ones((ONES_ROWS, tm), BF16)


def _mla_kernel(qt_ref, kc_ref, vt_ref, o_ref, sa_ref, sb_ref, m_ref, acc_ref, *, tile, heads):
    qi = pl.program_id(2)
    v_rows = V_HEAD_DIM + ONES_ROWS

    m_ref[...] = jnp.full_like(m_ref, -jnp.inf)
    acc_ref[...] = jnp.zeros_like(acc_ref)

    def scores(j, s_ref):
        rows = pl.ds(pl.multiple_of(j * tile, tile), tile)
        for h in range(heads):
            s_ref[h] = jnp.dot(kc_ref[0, rows, h * QK_PAD:(h + 1) * QK_PAD], qt_ref[0, h * QK_PAD:(h + 1) * QK_PAD, :],
                               preferred_element_type=F32)

    def consume(j, s_ref, masked):
        for h in range(heads):
            s = s_ref[h]
            if masked:
                k_idx = lax.broadcasted_iota(jnp.int32, s.shape, 0)
                q_idx = lax.broadcasted_iota(jnp.int32, s.shape, 1)
                s = jnp.where(k_idx <= q_idx, s, -jnp.inf)
            m_prev = m_ref[h]
            m_new = jnp.maximum(m_prev, jnp.max(s, axis=0, keepdims=True))
            alpha = jnp.exp2(m_prev - m_new)
            p = jnp.exp2(s - m_new)
            acc_ref[h] = alpha * acc_ref[h] + jnp.dot(vt_ref[0, j, h * v_rows:(h + 1) * v_rows, :], p.astype(BF16),
                                                      preferred_element_type=F32)
            m_ref[h] = m_new

    def pair(jj, carry):
        j = 2 * jj
        scores(j + 1, sb_ref)
        consume(j, sa_ref, False)
        scores(j + 2, sa_ref)
        consume(j + 1, sb_ref, False)
        return carry

    scores(0, sa_ref)
    lax.fori_loop(0, qi // 2, pair, 0)

    @pl.when(qi % 2 == 0)
    def _():
        consume(qi, sa_ref, True)

    @pl.when(qi % 2 == 1)
    def _():
        scores(qi, sb_ref)
        consume(qi - 1, sa_ref, False)
        consume(qi, sb_ref, True)

    for h in range(heads):
        out = acc_ref[h, :V_HEAD_DIM] / acc_ref[h, V_HEAD_DIM:V_HEAD_DIM + 1]
        o_ref[0, h * V_HEAD_DIM:(h + 1) * V_HEAD_DIM, :] = out.astype(o_ref.dtype)


def _out_proj_kernel(x_ref, attn_ref, ssm_ref, anw_ref, woa_ref, wos_ref, postw_ref, h_ref):
    a_t = attn_ref[0].astype(F32)
    scale = lax.rsqrt(jnp.mean(a_t * a_t, axis=0, keepdims=True) + EPS)
    attn_n = (a_t * scale * anw_ref[...]).astype(BF16).T
    mix = jnp.dot(attn_n, woa_ref[...], preferred_element_type=F32)
    mix = mix + jnp.dot(ssm_ref[...], wos_ref[...], preferred_element_type=F32)
    h_ref[...] = x_ref[...] + _rms(mix, postw_ref[...])


def _ffn_kernel(h_ref, prew_ref, wg_ref, wu_ref, wd_ref, postw_ref, o_ref, v_ref):
    f = pl.program_id(1)
    last = pl.num_programs(1) - 1
    tm = h_ref.shape[0]
    row_halves = (slice(0, tm // 2), slice(tm // 2, tm))

    def prenorm(rows):
        v_ref[rows, :] = _rms(h_ref[rows, :], prew_ref[...]).astype(BF16)
        o_ref[rows, :] = jnp.zeros((tm // 2, o_ref.shape[1]), F32)

    def swiglu(rows):
        v = v_ref[rows, :]
        half = wg_ref.shape[1] // 2
        acts = []
        for c in range(2):
            cols = slice(c * half, (c + 1) * half)
            gate = jnp.dot(v, wg_ref[:, cols], preferred_element_type=F32)
            up = jnp.dot(v, wu_ref[:, cols], preferred_element_type=F32)
            acts.append((_silu(gate) * up).astype(BF16))
        o_ref[rows, :] += jnp.dot(jnp.concatenate(acts, axis=1), wd_ref[...], preferred_element_type=F32)

    def finalize(rows):
        o_ref[rows, :] = h_ref[rows, :] + _rms(o_ref[rows, :], postw_ref[...])

    @pl.when(f == 0)
    def _():
        prenorm(row_halves[0])
        swiglu(row_halves[0])
        prenorm(row_halves[1])
        swiglu(row_halves[1])

    @pl.when((f > 0) & (f < last))
    def _():
        swiglu(slice(0, tm))

    @pl.when(f == last)
    def _():
        swiglu(row_halves[0])
        finalize(row_halves[0])
        swiglu(row_halves[1])
        finalize(row_halves[1])


def _tile(n, want):
    t = min(n, want)
    assert n % t == 0, (n, t)
    return t


def _resident(shape):
    return pl.BlockSpec(shape, lambda *_: (0,) * len(shape), pipeline_mode=pl.Buffered(1))


def _row(w):
    return w.reshape(1, -1).astype(F32)


def _layer(h, pos_row, p):
    b, s, d_model = h.shape
    m = b * s
    n_mla_heads = p["w_uq"].shape[1] // QK_HEAD_DIM
    mla_width = n_mla_heads * V_HEAD_DIM
    d_inner = p["ssd_norm_w"].shape[0]
    n_ssd_heads = p["dt_bias"].shape[0]
    d_xbc = d_inner + 2 * SSD_GROUPS * SSD_STATE
    d_ff = p["w_gate"].shape[1]
    T = SSD_CHUNK
    assert n_ssd_heads * SSD_HEAD_DIM == d_inner and n_ssd_heads <= LANES

    w_in = p["w_in"].T.astype(BF16)
    offs = np.cumsum([0, Q_LORA_RANK, KV_LORA_RANK, QK_ROPE_DIM, d_inner, d_xbc, n_ssd_heads])
    w_cq, w_ckv, w_kr, w_z, w_xbc, w_dt = [w_in[offs[i]:offs[i + 1]] for i in range(6)]
    zpad = lambda n: jnp.zeros((n, d_model), BF16)
    w_lat = w_in[:offs[2]]
    w_misc = jnp.concatenate([w_kr[:ROPE_HALF], zpad(ROPE_HALF), w_kr[ROPE_HALF:], zpad(ROPE_HALF),
                              w_dt, zpad(LANES - n_ssd_heads)], axis=0)

    wq3 = p["w_uq"].reshape(Q_LORA_RANK, n_mla_heads, QK_HEAD_DIM)
    zq = jnp.zeros((Q_LORA_RANK, n_mla_heads, ROPE_HALF), wq3.dtype)
    w_uq_r = jnp.concatenate([
        wq3[..., :QK_NOPE_DIM], wq3[..., QK_NOPE_DIM:QK_NOPE_DIM + ROPE_HALF], zq,
        wq3[..., QK_NOPE_DIM + ROPE_HALF:], zq], axis=-1).reshape(Q_LORA_RANK, n_mla_heads * QK_PAD).astype(BF16)
    w_uq_t = w_uq_r.T
    wkv3 = p["w_ukv"].astype(BF16).reshape(KV_LORA_RANK, n_mla_heads, QK_NOPE_DIM + V_HEAD_DIM)
    w_kn = wkv3[..., :QK_NOPE_DIM].reshape(KV_LORA_RANK, n_mla_heads * QK_NOPE_DIM)
    w_v_t = wkv3[..., QK_NOPE_DIM:].reshape(KV_LORA_RANK, mla_width).T

    inv_freq = ROPE_THETA ** (-jnp.arange(0, QK_ROPE_DIM, 2, dtype=F32) / QK_ROPE_DIM)
    z32 = jnp.zeros((ROPE_HALF,), F32)
    o32 = jnp.ones((ROPE_HALF,), F32)
    freq_tab = jnp.concatenate([inv_freq, z32, inv_freq, z32]).reshape(LANES, 1)
    sgn_tab = jnp.concatenate([-o32, z32, o32, z32]).reshape(LANES, 1)

    lane_pad = lambda v: jnp.pad(v.astype(F32), (0, LANES - n_ssd_heads)).reshape(1, LANES)
    a_neg = lane_pad(-jnp.exp(p["a_log"].astype(F32)) * LOG2_E)
    dt_bias = lane_pad(p["dt_bias"])
    d_skip_x = jnp.repeat(p["d_skip"].astype(F32), SSD_HEAD_DIM).reshape(1, d_inner)
    tri = jnp.tril(jnp.ones((T, T), BF16))
    expand = (jnp.arange(LANES)[:, None] == (jnp.arange(d_inner)[None, :] // SSD_HEAD_DIM)).astype(BF16)
    expand2 = jnp.concatenate([expand, expand], axis=0)

    tm = _tile(s, 512)
    assert tm % T == 0
    n_tiles = s // tm
    v_rows = V_HEAD_DIM + ONES_ROWS
    row_spec = lambda n: pl.BlockSpec((tm, n), lambda i: (i, 0))
    col_spec = lambda n: pl.BlockSpec((1, n, tm), lambda i: (i, 0, 0))
    q_t, kc, v_t, ssm = pl.pallas_call(
        functools.partial(_in_ssd_kernel, n_heads=n_mla_heads, q_scale=float(QK_HEAD_DIM) ** -0.5 * LOG2_E,
                          seq_tiles=n_tiles, d_inner=d_inner, n_ssd_heads=n_ssd_heads),
        grid=(m // tm,),
        in_specs=[row_spec(d_model), pl.BlockSpec((1, tm), lambda i: (0, i)), _resident((1, d_model)),
                  _resident((Q_LORA_RANK + KV_LORA_RANK, d_model)), _resident((d_inner, d_model)),
                  _resident((d_xbc, d_model)), _resident((2 * LANES, d_model)),
                  _resident((1, Q_LORA_RANK)), _resident((n_mla_heads * QK_PAD, Q_LORA_RANK)),
                  _resident((1, KV_LORA_RANK)), _resident((KV_LORA_RANK, n_mla_heads * QK_NOPE_DIM)),
                  _resident((mla_width, KV_LORA_RANK)),
                  _resident((LANES, 1)), _resident((LANES, 1)),
                  _resident((SSD_CONV, d_xbc)), _resident((1, d_xbc)),
                  _resident((1, LANES)), _resident((1, LANES)), _resident((1, d_inner)), _resident((1, d_inner)),
                  _resident((T, T)), _resident((2 * LANES, d_inner))],
        out_specs=[col_spec(n_mla_heads * QK_PAD), row_spec(n_mla_heads * QK_PAD), col_spec(n_mla_heads * v_rows),
                   row_spec(d_inner)],
        out_shape=[jax.ShapeDtypeStruct((m // tm, n_mla_heads * QK_PAD, tm), BF16),
                   jax.ShapeDtypeStruct((m, n_mla_heads * QK_PAD), BF16),
                   jax.ShapeDtypeStruct((m // tm, n_mla_heads * v_rows, tm), BF16),
                   jax.ShapeDtypeStruct((m, d_inner), BF16)],
        scratch_shapes=[pltpu.VMEM((CONV_HALO + tm, d_xbc), F32),
                        pltpu.VMEM((tm // T, 3 * T, 2 * LANES), BF16),
                        pltpu.VMEM((SSD_GROUPS, SSD_STATE, d_inner // SSD_GROUPS), F32)],
        compiler_params=pltpu.CompilerParams(dimension_semantics=("arbitrary",),
                                             vmem_limit_bytes=VMEM_LIMIT_BYTES),
        name="in_ssd",
    )(h.reshape(m, d_model), pos_row, _row(p["pre_mix_norm_w"]), w_lat, w_z, w_xbc,
      w_misc, _row(p["q_norm_w"]), w_uq_t, _row(p["kv_norm_w"]), w_kn, w_v_t, freq_tab, sgn_tab,
      p["conv_w"].astype(F32), _row(p["conv_b"]), dt_bias, a_neg, d_skip_x, _row(p["ssd_norm_w"]),
      tri, expand2)

    hp = next(c for c in (4, 2, 1) if n_mla_heads % c == 0)
    attn_t = pl.pallas_call(
        functools.partial(_mla_kernel, tile=tm, heads=hp),
        grid=(b, n_mla_heads // hp, n_tiles),
        in_specs=[pl.BlockSpec((1, hp * QK_PAD, tm), lambda bi, hi, qi: (bi * n_tiles + qi, hi, 0)),
                  pl.BlockSpec((1, s, hp * QK_PAD), lambda bi, hi, qi: (bi, 0, hi)),
                  pl.BlockSpec((1, n_tiles, hp * v_rows, tm), lambda bi, hi, qi: (bi, 0, hi, 0))],
        out_specs=pl.BlockSpec((1, hp * V_HEAD_DIM, tm), lambda bi, hi, qi: (bi * n_tiles + qi, hi, 0)),
        out_shape=jax.ShapeDtypeStruct((m // tm, mla_width, tm), BF16),
        scratch_shapes=[pltpu.VMEM((hp, tm, tm), F32), pltpu.VMEM((hp, tm, tm), F32), pltpu.VMEM((hp, 1, tm), F32),
                        pltpu.VMEM((hp, v_rows, tm), F32)],
        compiler_params=pltpu.CompilerParams(dimension_semantics=("arbitrary", "arbitrary", "arbitrary"),
                                             vmem_limit_bytes=VMEM_LIMIT_BYTES),
        name="mla",
    )(q_t, kc.reshape(b, s, -1), v_t.reshape(b, n_tiles, n_mla_heads * v_rows, tm))

    w_out = p["w_out"].astype(BF16)
    orow = lambda n: pl.BlockSpec((tm, n), lambda i: (i, 0))
    h1 = pl.pallas_call(
        _out_proj_kernel,
        grid=(m // tm,),
        in_specs=[orow(d_model), col_spec(mla_width), orow(d_inner), _resident((mla_width, 1)),
                  _resident((mla_width, d_model)), _resident((d_inner, d_model)), _resident((1, d_model))],
        out_specs=orow(d_model),
        out_shape=jax.ShapeDtypeStruct((m, d_model), F32),
        compiler_params=pltpu.CompilerParams(dimension_semantics=("arbitrary",),
                                             vmem_limit_bytes=VMEM_LIMIT_BYTES),
        name="out_proj",
    )(h.reshape(m, d_model), attn_t, ssm, p["attn_out_norm_w"].astype(F32).reshape(mla_width, 1),
      w_out[:mla_width], w_out[mla_width:], _row(p["post_mix_norm_w"]))

    tf_m = _tile(m, 1024)
    tf_f = _tile(d_ff, 512)
    assert d_ff // tf_f >= 2, "the FFN kernel's first and last d_ff steps must be distinct"
    out = pl.pallas_call(
        _ffn_kernel,
        grid=(m // tf_m, d_ff // tf_f),
        in_specs=[pl.BlockSpec((tf_m, d_model), lambda i, f: (i, 0)),
                  _resident((1, d_model)),
                  pl.BlockSpec((d_model, tf_f), lambda i, f: (0, f)),
                  pl.BlockSpec((d_model, tf_f), lambda i, f: (0, f)),
                  pl.BlockSpec((tf_f, d_model), lambda i, f: (f, 0)),
                  _resident((1, d_model))],
        out_specs=pl.BlockSpec((tf_m, d_model), lambda i, f: (i, 0)),
        out_shape=jax.ShapeDtypeStruct((m, d_model), F32),
        scratch_shapes=[pltpu.VMEM((tf_m, d_model), BF16)],
        compiler_params=pltpu.CompilerParams(dimension_semantics=("arbitrary", "arbitrary"),
                                             vmem_limit_bytes=VMEM_LIMIT_BYTES),
        name="ffn",
    )(h1, _row(p["pre_ffn_norm_w"]), p["w_gate"].astype(BF16), p["w_up"].astype(BF16),
      p["w_down"].astype(BF16), _row(p["post_ffn_norm_w"]))
    return out.reshape(b, s, d_model)


def kernel(x, positions, w_in, q_norm_w, w_uq, kv_norm_w, w_ukv, conv_w, conv_b, dt_bias, a_log, d_skip,
           ssd_norm_w, attn_out_norm_w, w_out, pre_mix_norm_w, post_mix_norm_w, pre_ffn_norm_w,
           post_ffn_norm_w, w_gate, w_up, w_down):
    stacked = dict(w_in=w_in, q_norm_w=q_norm_w, w_uq=w_uq, kv_norm_w=kv_norm_w, w_ukv=w_ukv, conv_w=conv_w,
                   conv_b=conv_b, dt_bias=dt_bias, a_log=a_log, d_skip=d_skip, ssd_norm_w=ssd_norm_w,
                   attn_out_norm_w=attn_out_norm_w, w_out=w_out, pre_mix_norm_w=pre_mix_norm_w,
                   post_mix_norm_w=post_mix_norm_w, pre_ffn_norm_w=pre_ffn_norm_w,
                   post_ffn_norm_w=post_ffn_norm_w, w_gate=w_gate, w_up=w_up, w_down=w_down)
    b, s, _ = x.shape
    pos_row = positions.astype(F32).reshape(1, b * s)
    h = x
    for l in range(w_in.shape[0]):
        h = _layer(h, pos_row, {k: v[l] for k, v in stacked.items()})
    return h
```

```python
import functools

import numpy as np
import jax
import jax.numpy as jnp
from jax import lax
from jax.experimental import pallas as pl
from jax.experimental.pallas import tpu as pltpu

F32 = jnp.float32
BF16 = jnp.bfloat16

V_HEAD_DIM = 128
QK_NOPE_DIM = 128
QK_ROPE_DIM = 64
QK_HEAD_DIM = QK_NOPE_DIM + QK_ROPE_DIM
Q_LORA_RANK = 512
KV_LORA_RANK = 512
ROPE_THETA = 10000.0
SSD_HEAD_DIM = 64
SSD_GROUPS = 2
SSD_STATE = 128
SSD_CONV = 4
SSD_CHUNK = 128
EPS = 1e-6

LANES = 128
SUBLANES = 8
VMEM_LIMIT_BYTES = 60 * 1024 * 1024

QK_PAD = 2 * LANES
ROPE_HALF = QK_ROPE_DIM // 2
LOG2_E = 1.4426950408889634
ONES_ROWS = 2 * SUBLANES
CONV_HALO = SUBLANES


def _rms(t, w):
    return t * lax.rsqrt(jnp.mean(t * t, axis=-1, keepdims=True) + EPS) * w


def _silu(t):
    h = 0.5 * t
    return h + h * jnp.tanh(h)


def _rope(t, cos, sin_signed):
    return t * cos + pltpu.roll(t, LANES // 2, axis=1) * sin_signed


def _ssd_chunk(k, ext_ref, hilo_ref, state_ref, z_c, dt_c, convw_ref, convb_ref, dtb_ref, aneg_ref,
               dskip_ref, nw_ref, tri_ref, e_ref, *, d_inner, n_heads):
    T, N, P, G = SSD_CHUNK, SSD_STATE, SSD_HEAD_DIM, SSD_GROUPS
    gw = d_inner // G

    window = ext_ref[k * T:k * T + CONV_HALO + T, :]
    conv = convb_ref[...] + convw_ref[SSD_CONV - 1:SSD_CONV, :] * window[CONV_HALO:]
    for tap in range(SSD_CONV - 1):
        shifted = pltpu.roll(window, SSD_CONV - 1 - tap, axis=0)[CONV_HALO:]
        conv = conv + convw_ref[tap:tap + 1, :] * shifted
    xbc = _silu(conv)
    xs = xbc[:, :d_inner]
    bm = xbc[:, d_inner:d_inner + G * N]
    cm = xbc[:, d_inner + G * N:]

    dt_in = dt_c + dtb_ref[...]
    dt = jnp.maximum(dt_in, 0.0) + jnp.log1p(jnp.exp(-jnp.abs(dt_in)))
    a = dt * aneg_ref[...]
    a_hi = a.astype(BF16)
    a_r = a - a_hi.astype(F32)
    a_mid = a_r.astype(BF16)
    a_lo = (a_r - a_mid.astype(F32)).astype(BF16)
    tri_b = tri_ref[...]
    a_cum = (jnp.dot(tri_b, a_hi, preferred_element_type=F32) + jnp.dot(tri_b, a_mid, preferred_element_type=F32)
             + jnp.dot(tri_b, a_lo, preferred_element_type=F32))
    a_cum_t = a_cum.T
    a_last = a_cum[T - 1:T, :]
    ea = jnp.exp2(a_cum)
    ds = jnp.exp2(a_last - a_cum)

    stacked = jnp.concatenate([dt, ea, ds], axis=0)
    s_hi = stacked.astype(BF16)
    hilo_ref[k, :, :LANES] = s_hi
    hilo_ref[k, :, LANES:] = (stacked - s_hi.astype(F32)).astype(BF16)
    expanded = jnp.dot(hilo_ref[k], e_ref[...], preferred_element_type=F32)
    dt_x, ea_x, ds_x = expanded[0:T], expanded[T:2 * T], expanded[2 * T:3 * T]

    xdt = xs * dt_x
    xdt_b = xdt.astype(BF16)
    xw_b = (xdt * ds_x).astype(BF16)

    row = lax.broadcasted_iota(jnp.int32, (T, T), 0)
    col = lax.broadcasted_iota(jnp.int32, (T, T), 1)
    tri = col <= row
    lane = lax.broadcasted_iota(jnp.int32, (T, 2 * P), 1)
    heads_per_group = n_heads // G

    y_parts = []
    y_off_parts = []
    for g in range(G):
        cols = slice(g * gw, (g + 1) * gw)
        bm_g = bm[:, g * N:(g + 1) * N]
        cm_g = cm[:, g * N:(g + 1) * N].astype(BF16)
        cb = lax.dot_general(cm_g, bm_g.astype(BF16), (((1,), (1,)), ((), ())), preferred_element_type=F32)
        for j in range(heads_per_group // 2):
            c0 = g * gw + 2 * j * P
            xp = xdt_b[:, c0:c0 + 2 * P]
            zero = jnp.zeros_like(xp)
            part = None
            for i, x_half in enumerate((jnp.where(lane < P, xp, zero), jnp.where(lane >= P, xp, zero))):
                h = g * heads_per_group + 2 * j + i
                seg = a_cum[:, h:h + 1] - a_cum_t[h:h + 1, :]
                w = (cb * jnp.exp2(jnp.where(tri, seg, -jnp.inf))).astype(BF16)
                d = jnp.dot(w, x_half, preferred_element_type=F32)
                part = d if part is None else part + d
            y_parts.append(part)
        prev = state_ref[g]
        y_off_parts.append(jnp.dot(cm_g, prev.astype(BF16), preferred_element_type=F32))
        new = jnp.dot(bm_g.T.astype(BF16), xw_b[:, cols], preferred_element_type=F32)
        state_ref[g] = prev * ea_x[T - 1:T, cols] + new
    y_diag = jnp.concatenate(y_parts, axis=1)
    y_off = jnp.concatenate(y_off_parts, axis=1) * ea_x

    y = y_diag + y_off + xs * dskip_ref[...]
    gated = y * _silu(z_c)
    outs = []
    for g in range(G):
        gg = gated[:, g * gw:(g + 1) * gw]
        outs.append(gg * lax.rsqrt(jnp.mean(gg * gg, axis=-1, keepdims=True) + EPS))
    return (jnp.concatenate(outs, axis=1) * nw_ref[...]).astype(BF16)


def _in_ssd_kernel(x_ref, pos_ref, wpre_ref, wlat_ref, wz_ref, wxbc_ref, wmisc_ref, qnw_ref, wuqt_ref, kvnw_ref,
                   wkn_ref, wvt_ref, freq_ref, sgn_ref, convw_ref, convb_ref, dtb_ref, aneg_ref, dskip_ref,
                   nw_ref, tri_ref, e_ref, qt_ref, kc_ref, vt_ref, ssm_ref, ext_ref, hilo_ref, state_ref,
                   *, n_heads, q_scale, seq_tiles, d_inner, n_ssd_heads):
    T = SSD_CHUNK
    tm = x_ref.shape[0]

    @pl.when(pl.program_id(0) % seq_tiles == 0)
    def _():
        ext_ref[0:CONV_HALO, :] = jnp.zeros((CONV_HALO, ext_ref.shape[1]), F32)
        state_ref[...] = jnp.zeros_like(state_ref)

    u = _rms(x_ref[...], wpre_ref[...]).astype(BF16)

    nt_dims = (((1,), (1,)), ((), ()))
    ext_ref[CONV_HALO:, :] = lax.dot_general(u, wxbc_ref[...], nt_dims, preferred_element_type=F32)
    z = lax.dot_general(u, wz_ref[...], nt_dims, preferred_element_type=F32)
    misc = lax.dot_general(u, wmisc_ref[...], nt_dims, preferred_element_type=F32)
    dt_raw = misc[:, LANES:]
    for k in range(tm // T):
        ssm_ref[k * T:(k + 1) * T, :] = _ssd_chunk(
            k, ext_ref, hilo_ref, state_ref, z[k * T:(k + 1) * T], dt_raw[k * T:(k + 1) * T],
            convw_ref, convb_ref, dtb_ref, aneg_ref, dskip_ref, nw_ref, tri_ref, e_ref,
            d_inner=d_inner, n_heads=n_ssd_heads)
    ext_ref[0:CONV_HALO, :] = ext_ref[tm:tm + CONV_HALO, :]

    ang_t = freq_ref[...] * pos_ref[...]
    cos_t = jnp.cos(ang_t)
    sin_t = jnp.sin(ang_t) * sgn_ref[...]
    k_rope = _rope(misc[:, :LANES], cos_t.T, sin_t.T).astype(BF16)

    cqkv = lax.dot_general(u, wlat_ref[...], nt_dims, preferred_element_type=F32)
    cqn = _rms(cqkv[:, :Q_LORA_RANK], qnw_ref[...]).astype(BF16)
    ckvn = _rms(cqkv[:, Q_LORA_RANK:], kvnw_ref[...]).astype(BF16)
    q_t = lax.dot_general(wuqt_ref[...], cqn, nt_dims, preferred_element_type=F32)
    cos_q = cos_t * q_scale
    sin_q = sin_t * q_scale
    half = LANES // 2
    for h in range(n_heads):
        base = h * QK_PAD
        qt_ref[0, base:base + LANES, :] = (q_t[base:base + LANES] * q_scale).astype(BF16)
        blk = q_t[base + LANES:base + QK_PAD]
        swapped = jnp.concatenate([blk[half:], blk[:half]], axis=0)
        qt_ref[0, base + LANES:base + QK_PAD, :] = (blk * cos_q + swapped * sin_q).astype(BF16)

    k_nope = jnp.dot(ckvn, wkn_ref[...], preferred_element_type=F32).astype(BF16)
    v_t = lax.dot_general(wvt_ref[...], ckvn, nt_dims, preferred_element_type=F32)
    v_rows = V_HEAD_DIM + ONES_ROWS
    for h in range(n_heads):
        kc_ref[:, h * QK_PAD:h * QK_PAD + LANES] = k_nope[:, h * LANES:(h + 1) * LANES]
        kc_ref[:, h * QK_PAD + LANES:(h + 1) * QK_PAD] = k_rope
        vt_ref[0, h * v_rows:h * v_rows + V_HEAD_DIM, :] = v_t[h * V_HEAD_DIM:(h + 1) * V_HEAD_DIM].astype(BF16)
        vt_ref[0, h * v_rows + V_HEAD_DIM:(h + 1) * v_rows, :] = jnp.ones((ONES_ROWS, tm), BF16)


def _mla_kernel(qt_ref, kc_ref, vt_ref, o_ref, sa_ref, sb_ref, m_ref, acc_ref, *, tile, heads):
    w = pl.program_id(2)
    v_rows = V_HEAD_DIM + ONES_ROWS
    chains = [(h, r) for h in range(heads) for r in range(2)]

    m_ref[...] = jnp.full_like(m_ref, -jnp.inf)
    acc_ref[...] = jnp.zeros_like(acc_ref)

    def scores(j, s_ref, only_r=None):
        rows = pl.ds(pl.multiple_of(j * tile, tile), tile)
        for c, (h, r) in enumerate(chains):
            if only_r is None or r == only_r:
                s_ref[c] = jnp.dot(kc_ref[0, rows, h * QK_PAD:(h + 1) * QK_PAD],
                                   qt_ref[r, h * QK_PAD:(h + 1) * QK_PAD, :], preferred_element_type=F32)

    def consume(j, s_ref, diag_r=None, only_r=None):
        for c, (h, r) in enumerate(chains):
            if only_r is not None and r != only_r:
                continue
            s = s_ref[c]
            if r == diag_r:
                k_idx = lax.broadcasted_iota(jnp.int32, s.shape, 0)
                q_idx = lax.broadcasted_iota(jnp.int32, s.shape, 1)
                s = jnp.where(k_idx <= q_idx, s, -jnp.inf)
            m_prev = m_ref[c]
            m_new = jnp.maximum(m_prev, jnp.max(s, axis=0, keepdims=True))
            alpha = jnp.exp2(m_prev - m_new)
            p = jnp.exp2(s - m_new)
            acc_ref[c] = alpha * acc_ref[c] + jnp.dot(vt_ref[0, j, h * v_rows:(h + 1) * v_rows, :], p.astype(BF16),
                                                      preferred_element_type=F32)
            m_ref[c] = m_new

    def pair(jj, carry):
        j = 2 * jj
        scores(j + 1, sb_ref)
        consume(j, sa_ref)
        scores(j + 2, sa_ref)
        consume(j + 1, sb_ref)
        return carry

    scores(0, sa_ref)
    lax.fori_loop(0, w, pair, 0)
    scores(2 * w + 1, sb_ref, only_r=1)
    consume(2 * w, sa_ref, diag_r=0)
    consume(2 * w + 1, sb_ref, diag_r=1, only_r=1)

    for c, (h, r) in enumerate(chains):
        out = acc_ref[c, :V_HEAD_DIM] / acc_ref[c, V_HEAD_DIM:V_HEAD_DIM + 1]
        o_ref[r, h * V_HEAD_DIM:(h + 1) * V_HEAD_DIM, :] = out.astype(o_ref.dtype)


def _out_proj_kernel(x_ref, attn_ref, ssm_ref, anw_ref, woa_ref, wos_ref, postw_ref, h_ref):
    a_t = attn_ref[0].astype(F32)
    scale = lax.rsqrt(jnp.mean(a_t * a_t, axis=0, keepdims=True) + EPS)
    attn_n = (a_t * scale * anw_ref[...]).astype(BF16).T
    mix = jnp.dot(attn_n, woa_ref[...], preferred_element_type=F32)
    mix = mix + jnp.dot(ssm_ref[...], wos_ref[...], preferred_element_type=F32)
    h_ref[...] = x_ref[...] + _rms(mix, postw_ref[...])


def _ffn_kernel(h_ref, prew_ref, wg_ref, wu_ref, wd_ref, postw_ref, o_ref, v_ref):
    f = pl.program_id(1)
    last = pl.num_programs(1) - 1
    tm = h_ref.shape[0]
    row_halves = (slice(0, tm // 2), slice(tm // 2, tm))

    def prenorm(rows):
        v_ref[rows, :] = _rms(h_ref[rows, :], prew_ref[...]).astype(BF16)
        o_ref[rows, :] = jnp.zeros((tm // 2, o_ref.shape[1]), F32)

    def swiglu(rows):
        v = v_ref[rows, :]
        half = wg_ref.shape[1] // 2
        acts = []
        for c in range(2):
            cols = slice(c * half, (c + 1) * half)
            gate = jnp.dot(v, wg_ref[:, cols], preferred_element_type=F32)
            up = jnp.dot(v, wu_ref[:, cols], preferred_element_type=F32)
            acts.append((_silu(gate) * up).astype(BF16))
        o_ref[rows, :] += jnp.dot(jnp.concatenate(acts, axis=1), wd_ref[...], preferred_element_type=F32)

    def finalize(rows):
        o_ref[rows, :] = h_ref[rows, :] + _rms(o_ref[rows, :], postw_ref[...])

    @pl.when(f == 0)
    def _():
        prenorm(row_halves[0])
        swiglu(row_halves[0])
        prenorm(row_halves[1])
        swiglu(row_halves[1])

    @pl.when((f > 0) & (f < last))
    def _():
        swiglu(slice(0, tm))

    @pl.when(f == last)
    def _():
        swiglu(row_halves[0])
        finalize(row_halves[0])
        swiglu(row_halves[1])
        finalize(row_halves[1])


def _tile(n, want):
    t = min(n, want)
    assert n % t == 0, (n, t)
    return t


def _resident(shape):
    return pl.BlockSpec(shape, lambda *_: (0,) * len(shape), pipeline_mode=pl.Buffered(1))


def _row(w):
    return w.reshape(1, -1).astype(F32)


def _layer(h, pos_row, p):
    b, s, d_model = h.shape
    m = b * s
    n_mla_heads = p["w_uq"].shape[1] // QK_HEAD_DIM
    mla_width = n_mla_heads * V_HEAD_DIM
    d_inner = p["ssd_norm_w"].shape[0]
    n_ssd_heads = p["dt_bias"].shape[0]
    d_xbc = d_inner + 2 * SSD_GROUPS * SSD_STATE
    d_ff = p["w_gate"].shape[1]
    T = SSD_CHUNK
    assert n_ssd_heads * SSD_HEAD_DIM == d_inner and n_ssd_heads <= LANES

    w_in = p["w_in"].T.astype(BF16)
    offs = np.cumsum([0, Q_LORA_RANK, KV_LORA_RANK, QK_ROPE_DIM, d_inner, d_xbc, n_ssd_heads])
    w_cq, w_ckv, w_kr, w_z, w_xbc, w_dt = [w_in[offs[i]:offs[i + 1]] for i in range(6)]
    zpad = lambda n: jnp.zeros((n, d_model), BF16)
    w_lat = w_in[:offs[2]]
    w_misc = jnp.concatenate([w_kr[:ROPE_HALF], zpad(ROPE_HALF), w_kr[ROPE_HALF:], zpad(ROPE_HALF),
                              w_dt, zpad(LANES - n_ssd_heads)], axis=0)

    wq3 = p["w_uq"].reshape(Q_LORA_RANK, n_mla_heads, QK_HEAD_DIM)
    zq = jnp.zeros((Q_LORA_RANK, n_mla_heads, ROPE_HALF), wq3.dtype)
    w_uq_r = jnp.concatenate([
        wq3[..., :QK_NOPE_DIM], wq3[..., QK_NOPE_DIM:QK_NOPE_DIM + ROPE_HALF], zq,
        wq3[..., QK_NOPE_DIM + ROPE_HALF:], zq], axis=-1).reshape(Q_LORA_RANK, n_mla_heads * QK_PAD).astype(BF16)
    w_uq_t = w_uq_r.T
    wkv3 = p["w_ukv"].astype(BF16).reshape(KV_LORA_RANK, n_mla_heads, QK_NOPE_DIM + V_HEAD_DIM)
    w_kn = wkv3[..., :QK_NOPE_DIM].reshape(KV_LORA_RANK, n_mla_heads * QK_NOPE_DIM)
    w_v_t = wkv3[..., QK_NOPE_DIM:].reshape(KV_LORA_RANK, mla_width).T

    inv_freq = ROPE_THETA ** (-jnp.arange(0, QK_ROPE_DIM, 2, dtype=F32) / QK_ROPE_DIM)
    z32 = jnp.zeros((ROPE_HALF,), F32)
    o32 = jnp.ones((ROPE_HALF,), F32)
    freq_tab = jnp.concatenate([inv_freq, z32, inv_freq, z32]).reshape(LANES, 1)
    sgn_tab = jnp.concatenate([-o32, z32, o32, z32]).reshape(LANES, 1)

    lane_pad = lambda v: jnp.pad(v.astype(F32), (0, LANES - n_ssd_heads)).reshape(1, LANES)
    a_neg = lane_pad(-jnp.exp(p["a_log"].astype(F32)) * LOG2_E)
    dt_bias = lane_pad(p["dt_bias"])
    d_skip_x = jnp.repeat(p["d_skip"].astype(F32), SSD_HEAD_DIM).reshape(1, d_inner)
    tri = jnp.tril(jnp.ones((T, T), BF16))
    expand = (jnp.arange(LANES)[:, None] == (jnp.arange(d_inner)[None, :] // SSD_HEAD_DIM)).astype(BF16)
    expand2 = jnp.concatenate([expand, expand], axis=0)

    tm = _tile(s, 512)
    assert tm % T == 0
    n_tiles = s // tm
    v_rows = V_HEAD_DIM + ONES_ROWS
    row_spec = lambda n: pl.BlockSpec((tm, n), lambda i: (i, 0))
    col_spec = lambda n: pl.BlockSpec((1, n, tm), lambda i: (i, 0, 0))
    q_t, kc, v_t, ssm = pl.pallas_call(
        functools.partial(_in_ssd_kernel, n_heads=n_mla_heads, q_scale=float(QK_HEAD_DIM) ** -0.5 * LOG2_E,
                          seq_tiles=n_tiles, d_inner=d_inner, n_ssd_heads=n_ssd_heads),
        grid=(m // tm,),
        in_specs=[row_spec(d_model), pl.BlockSpec((1, tm), lambda i: (0, i)), _resident((1, d_model)),
                  _resident((Q_LORA_RANK + KV_LORA_RANK, d_model)), _resident((d_inner, d_model)),
                  _resident((d_xbc, d_model)), _resident((2 * LANES, d_model)),
                  _resident((1, Q_LORA_RANK)), _resident((n_mla_heads * QK_PAD, Q_LORA_RANK)),
                  _resident((1, KV_LORA_RANK)), _resident((KV_LORA_RANK, n_mla_heads * QK_NOPE_DIM)),
                  _resident((mla_width, KV_LORA_RANK)),
                  _resident((LANES, 1)), _resident((LANES, 1)),
                  _resident((SSD_CONV, d_xbc)), _resident((1, d_xbc)),
                  _resident((1, LANES)), _resident((1, LANES)), _resident((1, d_inner)), _resident((1, d_inner)),
                  _resident((T, T)), _resident((2 * LANES, d_inner))],
        out_specs=[col_spec(n_mla_heads * QK_PAD), row_spec(n_mla_heads * QK_PAD), col_spec(n_mla_heads * v_rows),
                   row_spec(d_inner)],
        out_shape=[jax.ShapeDtypeStruct((m // tm, n_mla_heads * QK_PAD, tm), BF16),
                   jax.ShapeDtypeStruct((m, n_mla_heads * QK_PAD), BF16),
                   jax.ShapeDtypeStruct((m // tm, n_mla_heads * v_rows, tm), BF16),
                   jax.ShapeDtypeStruct((m, d_inner), BF16)],
        scratch_shapes=[pltpu.VMEM((CONV_HALO + tm, d_xbc), F32),
                        pltpu.VMEM((tm // T, 3 * T, 2 * LANES), BF16),
                        pltpu.VMEM((SSD_GROUPS, SSD_STATE, d_inner // SSD_GROUPS), F32)],
        compiler_params=pltpu.CompilerParams(dimension_semantics=("arbitrary",),
                                             vmem_limit_bytes=VMEM_LIMIT_BYTES),
        name="in_ssd",
    )(h.reshape(m, d_model), pos_row, _row(p["pre_mix_norm_w"]), w_lat, w_z, w_xbc,
      w_misc, _row(p["q_norm_w"]), w_uq_t, _row(p["kv_norm_w"]), w_kn, w_v_t, freq_tab, sgn_tab,
      p["conv_w"].astype(F32), _row(p["conv_b"]), dt_bias, a_neg, d_skip_x, _row(p["ssd_norm_w"]),
      tri, expand2)

    hp = next(c for c in (4, 2, 1) if n_mla_heads % c == 0)
    assert n_tiles % 2 == 0, "the attention kernel takes query tiles in pairs"
    n_wide = n_tiles // 2
    attn_t = pl.pallas_call(
        functools.partial(_mla_kernel, tile=tm, heads=hp),
        grid=(b, n_mla_heads // hp, n_wide),
        in_specs=[pl.BlockSpec((2, hp * QK_PAD, tm), lambda bi, hi, wi: (bi * n_wide + wi, hi, 0)),
                  pl.BlockSpec((1, s, hp * QK_PAD), lambda bi, hi, wi: (bi, 0, hi)),
                  pl.BlockSpec((1, n_tiles, hp * v_rows, tm), lambda bi, hi, wi: (bi, 0, hi, 0))],
        out_specs=pl.BlockSpec((2, hp * V_HEAD_DIM, tm), lambda bi, hi, wi: (bi * n_wide + wi, hi, 0)),
        out_shape=jax.ShapeDtypeStruct((m // tm, mla_width, tm), BF16),
        scratch_shapes=[pltpu.VMEM((2 * hp, tm, tm), F32), pltpu.VMEM((2 * hp, tm, tm), F32),
                        pltpu.VMEM((2 * hp, 1, tm), F32), pltpu.VMEM((2 * hp, v_rows, tm), F32)],
        compiler_params=pltpu.CompilerParams(dimension_semantics=("arbitrary", "arbitrary", "arbitrary"),
                                             vmem_limit_bytes=VMEM_LIMIT_BYTES),
        name="mla",
    )(q_t, kc.reshape(b, s, -1), v_t.reshape(b, n_tiles, n_mla_heads * v_rows, tm))

    w_out = p["w_out"].astype(BF16)
    orow = lambda n: pl.BlockSpec((tm, n), lambda i: (i, 0))
    h1 = pl.pallas_call(
        _out_proj_kernel,
        grid=(m // tm,),
        in_specs=[orow(d_model), col_spec(mla_width), orow(d_inner), _resident((mla_width, 1)),
                  _resident((mla_width, d_model)), _resident((d_inner, d_model)), _resident((1, d_model))],
        out_specs=orow(d_model),
        out_shape=jax.ShapeDtypeStruct((m, d_model), F32),
        compiler_params=pltpu.CompilerParams(dimension_semantics=("arbitrary",),
                                             vmem_limit_bytes=VMEM_LIMIT_BYTES),
        name="out_proj",
    )(h.reshape(m, d_model), attn_t, ssm, p["attn_out_norm_w"].astype(F32).reshape(mla_width, 1),
      w_out[:mla_width], w_out[mla_width:], _row(p["post_mix_norm_w"]))

    tf_m = _tile(m, 1024)
    tf_f = _tile(d_ff, 512)
    assert d_ff // tf_f >= 2, "the FFN kernel's first and last d_ff steps must be distinct"
    out = pl.pallas_call(
        _ffn_kernel,
        grid=(m // tf_m, d_ff // tf_f),
        in_specs=[pl.BlockSpec((tf_m, d_model), lambda i, f: (i, 0)),
                  _resident((1, d_model)),
                  pl.BlockSpec((d_model, tf_f), lambda i, f: (0, f)),
                  pl.BlockSpec((d_model, tf_f), lambda i, f: (0, f)),
                  pl.BlockSpec((tf_f, d_model), lambda i, f: (f, 0)),
                  _resident((1, d_model))],
        out_specs=pl.BlockSpec((tf_m, d_model), lambda i, f: (i, 0)),
        out_shape=jax.ShapeDtypeStruct((m, d_model), F32),
        scratch_shapes=[pltpu.VMEM((tf_m, d_model), BF16)],
        compiler_params=pltpu.CompilerParams(dimension_semantics=("arbitrary", "arbitrary"),
                                             vmem_limit_bytes=VMEM_LIMIT_BYTES),
        name="ffn",
    )(h1, _row(p["pre_ffn_norm_w"]), p["w_gate"].astype(BF16), p["w_up"].astype(BF16),
      p["w_down"].astype(BF16), _row(p["post_ffn_norm_w"]))
    return out.reshape(b, s, d_model)


def kernel(x, positions, w_in, q_norm_w, w_uq, kv_norm_w, w_ukv, conv_w, conv_b, dt_bias, a_log, d_skip,
           ssd_norm_w, attn_out_norm_w, w_out, pre_mix_norm_w, post_mix_norm_w, pre_ffn_norm_w,
           post_ffn_norm_w, w_gate, w_up, w_down):
    stacked = dict(w_in=w_in, q_norm_w=q_norm_w, w_uq=w_uq, kv_norm_w=kv_norm_w, w_ukv=w_ukv, conv_w=conv_w,
                   conv_b=conv_b, dt_bias=dt_bias, a_log=a_log, d_skip=d_skip, ssd_norm_w=ssd_norm_w,
                   attn_out_norm_w=attn_out_norm_w, w_out=w_out, pre_mix_norm_w=pre_mix_norm_w,
                   post_mix_norm_w=post_mix_norm_w, pre_ffn_norm_w=pre_ffn_norm_w,
                   post_ffn_norm_w=post_ffn_norm_w, w_gate=w_gate, w_up=w_up, w_down=w_down)
    b, s, _ = x.shape
    pos_row = positions.astype(F32).reshape(1, b * s)
    h = x
    for l in range(w_in.shape[0]):
        h = _layer(h, pos_row, {k: v[l] for k, v in stacked.items()})
    return h
```

```python
import functools

import numpy as np
import jax
import jax.numpy as jnp
from jax import lax
from jax.experimental import pallas as pl
from jax.experimental.pallas import tpu as pltpu

F32 = jnp.float32
BF16 = jnp.bfloat16

V_HEAD_DIM = 128
QK_NOPE_DIM = 128
QK_ROPE_DIM = 64
QK_HEAD_DIM = QK_NOPE_DIM + QK_ROPE_DIM
Q_LORA_RANK = 512
KV_LORA_RANK = 512
ROPE_THETA = 10000.0
SSD_HEAD_DIM = 64
SSD_GROUPS = 2
SSD_STATE = 128
SSD_CONV = 4
SSD_CHUNK = 128
EPS = 1e-6

LANES = 128
SUBLANES = 8
VMEM_LIMIT_BYTES = 60 * 1024 * 1024

QK_PAD = 2 * LANES
ROPE_HALF = QK_ROPE_DIM // 2
LOG2_E = 1.4426950408889634
ONES_ROWS = 2 * SUBLANES
CONV_HALO = SUBLANES


def _rms(t, w):
    return t * lax.rsqrt(jnp.mean(t * t, axis=-1, keepdims=True) + EPS) * w


def _silu(t):
    h = 0.5 * t
    return h + h * jnp.tanh(h)


def _rope(t, cos, sin_signed):
    return t * cos + pltpu.roll(t, LANES // 2, axis=1) * sin_signed


def _ssd_chunk(k, ext_ref, hilo_ref, state_ref, z_c, dt_c, convw_ref, convb_ref, dtb_ref, aneg_ref,
               dskip_ref, nw_ref, tri_ref, e_ref, *, d_inner, n_heads):
    T, N, P, G = SSD_CHUNK, SSD_STATE, SSD_HEAD_DIM, SSD_GROUPS
    gw = d_inner // G

    window = ext_ref[k * T:k * T + CONV_HALO + T, :]
    conv = convb_ref[...] + convw_ref[SSD_CONV - 1:SSD_CONV, :] * window[CONV_HALO:]
    for tap in range(SSD_CONV - 1):
        shifted = pltpu.roll(window, SSD_CONV - 1 - tap, axis=0)[CONV_HALO:]
        conv = conv + convw_ref[tap:tap + 1, :] * shifted
    xbc = _silu(conv)
    xs = xbc[:, :d_inner]
    bm = xbc[:, d_inner:d_inner + G * N]
    cm = xbc[:, d_inner + G * N:]

    dt_in = dt_c + dtb_ref[...]
    dt = jnp.maximum(dt_in, 0.0) + jnp.log1p(jnp.exp(-jnp.abs(dt_in)))
    a = dt * aneg_ref[...]
    a_hi = a.astype(BF16)
    a_r = a - a_hi.astype(F32)
    a_mid = a_r.astype(BF16)
    a_lo = (a_r - a_mid.astype(F32)).astype(BF16)
    tri_b = tri_ref[...]
    a_cum = (jnp.dot(tri_b, a_hi, preferred_element_type=F32) + jnp.dot(tri_b, a_mid, preferred_element_type=F32)
             + jnp.dot(tri_b, a_lo, preferred_element_type=F32))
    a_cum_t = a_cum.T
    a_last = a_cum[T - 1:T, :]
    ea = jnp.exp2(a_cum)
    ds = jnp.exp2(a_last - a_cum)

    stacked = jnp.concatenate([dt, ea, ds], axis=0)
    s_hi = stacked.astype(BF16)
    hilo_ref[k, :, :LANES] = s_hi
    hilo_ref[k, :, LANES:] = (stacked - s_hi.astype(F32)).astype(BF16)
    expanded = jnp.dot(hilo_ref[k], e_ref[...], preferred_element_type=F32)
    dt_x, ea_x, ds_x = expanded[0:T], expanded[T:2 * T], expanded[2 * T:3 * T]

    xdt = xs * dt_x
    xdt_b = xdt.astype(BF16)
    xw_b = (xdt * ds_x).astype(BF16)

    row = lax.broadcasted_iota(jnp.int32, (T, T), 0)
    col = lax.broadcasted_iota(jnp.int32, (T, T), 1)
    tri = col <= row
    lane = lax.broadcasted_iota(jnp.int32, (T, 2 * P), 1)
    heads_per_group = n_heads // G

    y_parts = []
    y_off_parts = []
    for g in range(G):
        cols = slice(g * gw, (g + 1) * gw)
        bm_g = bm[:, g * N:(g + 1) * N]
        cm_g = cm[:, g * N:(g + 1) * N].astype(BF16)
        cb = lax.dot_general(cm_g, bm_g.astype(BF16), (((1,), (1,)), ((), ())), preferred_element_type=F32)
        for j in range(heads_per_group // 2):
            c0 = g * gw + 2 * j * P
            xp = xdt_b[:, c0:c0 + 2 * P]
            zero = jnp.zeros_like(xp)
            part = None
            for i, x_half in enumerate((jnp.where(lane < P, xp, zero), jnp.where(lane >= P, xp, zero))):
                h = g * heads_per_group + 2 * j + i
                seg = a_cum[:, h:h + 1] - a_cum_t[h:h + 1, :]
                w = (cb * jnp.exp2(jnp.where(tri, seg, -jnp.inf))).astype(BF16)
                d = jnp.dot(w, x_half, preferred_element_type=F32)
                part = d if part is None else part + d
            y_parts.append(part)
        prev = state_ref[g]
        y_off_parts.append(jnp.dot(cm_g, prev.astype(BF16), preferred_element_type=F32))
        new = jnp.dot(bm_g.T.astype(BF16), xw_b[:, cols], preferred_element_type=F32)
        state_ref[g] = prev * ea_x[T - 1:T, cols] + new
    y_diag = jnp.concatenate(y_parts, axis=1)
    y_off = jnp.concatenate(y_off_parts, axis=1) * ea_x

    y = y_diag + y_off + xs * dskip_ref[...]
    gated = y * _silu(z_c)
    outs = []
    for g in range(G):
        gg = gated[:, g * gw:(g + 1) * gw]
        outs.append(gg * lax.rsqrt(jnp.mean(gg * gg, axis=-1, keepdims=True) + EPS))
    return (jnp.concatenate(outs, axis=1) * nw_ref[...]).astype(BF16)


def _in_ssd_kernel(x_ref, pos_ref, wpre_ref, wlat_ref, wz_ref, wxbc_ref, wmisc_ref, qnw_ref, wuqt_ref, kvnw_ref,
                   wkn_ref, wvt_ref, freq_ref, sgn_ref, convw_ref, convb_ref, dtb_ref, aneg_ref, dskip_ref,
                   nw_ref, tri_ref, e_ref, qt_ref, kc_ref, vt_ref, ssm_ref, ext_ref, hilo_ref, state_ref,
                   *, n_heads, q_scale, seq_tiles, d_inner, n_ssd_heads):
    T = SSD_CHUNK
    tm = x_ref.shape[0]

    @pl.when(pl.program_id(0) % seq_tiles == 0)
    def _():
        ext_ref[0:CONV_HALO, :] = jnp.zeros((CONV_HALO, ext_ref.shape[1]), F32)
        state_ref[...] = jnp.zeros_like(state_ref)

    u = _rms(x_ref[...], wpre_ref[...]).astype(BF16)

    nt_dims = (((1,), (1,)), ((), ()))
    ext_ref[CONV_HALO:, :] = lax.dot_general(u, wxbc_ref[...], nt_dims, preferred_element_type=F32)
    z = lax.dot_general(u, wz_ref[...], nt_dims, preferred_element_type=F32)
    misc = lax.dot_general(u, wmisc_ref[...], nt_dims, preferred_element_type=F32)
    dt_raw = misc[:, LANES:]
    for k in range(tm // T):
        ssm_ref[k * T:(k + 1) * T, :] = _ssd_chunk(
            k, ext_ref, hilo_ref, state_ref, z[k * T:(k + 1) * T], dt_raw[k * T:(k + 1) * T],
            convw_ref, convb_ref, dtb_ref, aneg_ref, dskip_ref, nw_ref, tri_ref, e_ref,
            d_inner=d_inner, n_heads=n_ssd_heads)
    ext_ref[0:CONV_HALO, :] = ext_ref[tm:tm + CONV_HALO, :]

    ang_t = freq_ref[...] * pos_ref[...]
    cos_t = jnp.cos(ang_t)
    sin_t = jnp.sin(ang_t) * sgn_ref[...]
    k_rope = _rope(misc[:, :LANES], cos_t.T, sin_t.T).astype(BF16)

    cqkv = lax.dot_general(u, wlat_ref[...], nt_dims, preferred_element_type=F32)
    cqn = _rms(cqkv[:, :Q_LORA_RANK], qnw_ref[...]).astype(BF16)
    ckvn = _rms(cqkv[:, Q_LORA_RANK:], kvnw_ref[...]).astype(BF16)
    q_t = lax.dot_general(wuqt_ref[...], cqn, nt_dims, preferred_element_type=F32)
    cos_q = cos_t * q_scale
    sin_q = sin_t * q_scale
    half = LANES // 2
    for h in range(n_heads):
        base = h * QK_PAD
        qt_ref[0, base:base + LANES, :] = (q_t[base:base + LANES] * q_scale).astype(BF16)
        blk = q_t[base + LANES:base + QK_PAD]
        swapped = jnp.concatenate([blk[half:], blk[:half]], axis=0)
        qt_ref[0, base + LANES:base + QK_PAD, :] = (blk * cos_q + swapped * sin_q).astype(BF16)

    k_nope = jnp.dot(ckvn, wkn_ref[...], preferred_element_type=F32).astype(BF16)
    v_t = lax.dot_general(wvt_ref[...], ckvn, nt_dims, preferred_element_type=F32)
    v_rows = V_HEAD_DIM + ONES_ROWS
    for h in range(n_heads):
        kc_ref[:, h * QK_PAD:h * QK_PAD + LANES] = k_nope[:, h * LANES:(h + 1) * LANES]
        kc_ref[:, h * QK_PAD + LANES:(h + 1) * QK_PAD] = k_rope
        vt_ref[0, h * v_rows:h * v_rows + V_HEAD_DIM, :] = v_t[h * V_HEAD_DIM:(h + 1) * V_HEAD_DIM].astype(BF16)
        vt_ref[0, h * v_rows + V_HEAD_DIM:(h + 1) * v_rows, :] = jnp.ones((ONES_ROWS, tm), BF16)


def _mla_kernel(qt_ref, kc_ref, vt_ref, o_ref, sa_ref, sb_ref, m_ref, acc_ref, *, tile, heads):
    w = pl.program_id(2)
    v_rows = V_HEAD_DIM + ONES_ROWS
    chains = [(h, r) for h in range(heads) for r in range(2)]

    m_ref[...] = jnp.full_like(m_ref, -jnp.inf)
    acc_ref[...] = jnp.zeros_like(acc_ref)

    def scores(j, s_ref, only_r=None):
        rows = pl.ds(pl.multiple_of(j * tile, tile), tile)
        for c, (h, r) in enumerate(chains):
            if only_r is None or r == only_r:
                s_ref[c] = jnp.dot(kc_ref[0, rows, h * QK_PAD:(h + 1) * QK_PAD],
                                   qt_ref[r, h * QK_PAD:(h + 1) * QK_PAD, :], preferred_element_type=F32)

    def consume(j, s_ref, diag_r=None, only_r=None):
        for c, (h, r) in enumerate(chains):
            if only_r is not None and r != only_r:
                continue
            s = s_ref[c]
            if r == diag_r:
                k_idx = lax.broadcasted_iota(jnp.int32, s.shape, 0)
                q_idx = lax.broadcasted_iota(jnp.int32, s.shape, 1)
                s = jnp.where(k_idx <= q_idx, s, -jnp.inf)
            m_prev = m_ref[c]
            m_new = jnp.maximum(m_prev, jnp.max(s, axis=0, keepdims=True))
            alpha = jnp.exp2(m_prev - m_new)
            p = jnp.exp2(s - m_new)
            acc_ref[c] = alpha * acc_ref[c] + jnp.dot(vt_ref[0, j, h * v_rows:(h + 1) * v_rows, :], p.astype(BF16),
                                                      preferred_element_type=F32)
            m_ref[c] = m_new

    def pair(jj, carry):
        j = 2 * jj
        scores(j + 1, sb_ref)
        consume(j, sa_ref)
        scores(j + 2, sa_ref)
        consume(j + 1, sb_ref)
        return carry

    scores(0, sa_ref)
    lax.fori_loop(0, w, pair, 0)
    scores(2 * w + 1, sb_ref, only_r=1)
    consume(2 * w, sa_ref, diag_r=0)
    consume(2 * w + 1, sb_ref, diag_r=1, only_r=1)

    for c, (h, r) in enumerate(chains):
        out = acc_ref[c, :V_HEAD_DIM] / acc_ref[c, V_HEAD_DIM:V_HEAD_DIM + 1]
        o_ref[r, h * V_HEAD_DIM:(h + 1) * V_HEAD_DIM, :] = out.astype(o_ref.dtype)


def _out_proj_kernel(x_ref, attn_ref, ssm_ref, anw_ref, woa_ref, wos_ref, postw_ref, h_ref):
    a_t = attn_ref[0].astype(F32)
    scale = lax.rsqrt(jnp.mean(a_t * a_t, axis=0, keepdims=True) + EPS)
    attn_n = (a_t * scale * anw_ref[...]).astype(BF16).T
    mix = jnp.dot(attn_n, woa_ref[...], preferred_element_type=F32)
    mix = mix + jnp.dot(ssm_ref[...], wos_ref[...], preferred_element_type=F32)
    h_ref[...] = x_ref[...] + _rms(mix, postw_ref[...])


def _ffn_kernel(h_ref, prew_ref, wg_ref, wu_ref, wd_ref, postw_ref, o_ref, v_ref):
    f = pl.program_id(1)
    last = pl.num_programs(1) - 1
    tm = h_ref.shape[0]
    row_halves = (slice(0, tm // 2), slice(tm // 2, tm))

    def prenorm(rows):
        v_ref[rows, :] = _rms(h_ref[rows, :], prew_ref[...]).astype(BF16)
        o_ref[rows, :] = jnp.zeros((tm // 2, o_ref.shape[1]), F32)

    def swiglu(rows):
        v = v_ref[rows, :]
        half = wg_ref.shape[1] // 2
        acts = []
        for c in range(2):
            cols = slice(c * half, (c + 1) * half)
            gate = jnp.dot(v, wg_ref[:, cols], preferred_element_type=F32)
            up = jnp.dot(v, wu_ref[:, cols], preferred_element_type=F32)
            acts.append((_silu(gate) * up).astype(BF16))
        o_ref[rows, :] += jnp.dot(jnp.concatenate(acts, axis=1), wd_ref[...].astype(BF16),
                                  preferred_element_type=F32)

    def finalize(rows):
        o_ref[rows, :] = h_ref[rows, :] + _rms(o_ref[rows, :], postw_ref[...])

    @pl.when(f == 0)
    def _():
        prenorm(row_halves[0])
        swiglu(row_halves[0])
        prenorm(row_halves[1])
        swiglu(row_halves[1])

    @pl.when((f > 0) & (f < last))
    def _():
        swiglu(slice(0, tm))

    @pl.when(f == last)
    def _():
        swiglu(row_halves[0])
        finalize(row_halves[0])
        swiglu(row_halves[1])
        finalize(row_halves[1])


def _tile(n, want):
    t = min(n, want)
    assert n % t == 0, (n, t)
    return t


def _resident(shape):
    return pl.BlockSpec(shape, lambda *_: (0,) * len(shape), pipeline_mode=pl.Buffered(1))


def _row(w):
    return w.reshape(1, -1).astype(F32)


def _layer(h, pos_row, p):
    b, s, d_model = h.shape
    m = b * s
    n_mla_heads = p["w_uq"].shape[1] // QK_HEAD_DIM
    mla_width = n_mla_heads * V_HEAD_DIM
    d_inner = p["ssd_norm_w"].shape[0]
    n_ssd_heads = p["dt_bias"].shape[0]
    d_xbc = d_inner + 2 * SSD_GROUPS * SSD_STATE
    d_ff = p["w_gate"].shape[1]
    T = SSD_CHUNK
    assert n_ssd_heads * SSD_HEAD_DIM == d_inner and n_ssd_heads <= LANES

    w_in = p["w_in"].T.astype(BF16)
    offs = np.cumsum([0, Q_LORA_RANK, KV_LORA_RANK, QK_ROPE_DIM, d_inner, d_xbc, n_ssd_heads])
    w_cq, w_ckv, w_kr, w_z, w_xbc, w_dt = [w_in[offs[i]:offs[i + 1]] for i in range(6)]
    zpad = lambda n: jnp.zeros((n, d_model), BF16)
    w_lat = w_in[:offs[2]]
    w_misc = jnp.concatenate([w_kr[:ROPE_HALF], zpad(ROPE_HALF), w_kr[ROPE_HALF:], zpad(ROPE_HALF),
                              w_dt, zpad(LANES - n_ssd_heads)], axis=0)

    wq3 = p["w_uq"].reshape(Q_LORA_RANK, n_mla_heads, QK_HEAD_DIM)
    zq = jnp.zeros((Q_LORA_RANK, n_mla_heads, ROPE_HALF), wq3.dtype)
    w_uq_r = jnp.concatenate([
        wq3[..., :QK_NOPE_DIM], wq3[..., QK_NOPE_DIM:QK_NOPE_DIM + ROPE_HALF], zq,
        wq3[..., QK_NOPE_DIM + ROPE_HALF:], zq], axis=-1).reshape(Q_LORA_RANK, n_mla_heads * QK_PAD).astype(BF16)
    w_uq_t = w_uq_r.T
    wkv3 = p["w_ukv"].astype(BF16).reshape(KV_LORA_RANK, n_mla_heads, QK_NOPE_DIM + V_HEAD_DIM)
    w_kn = wkv3[..., :QK_NOPE_DIM].reshape(KV_LORA_RANK, n_mla_heads * QK_NOPE_DIM)
    w_v_t = wkv3[..., QK_NOPE_DIM:].reshape(KV_LORA_RANK, mla_width).T

    inv_freq = ROPE_THETA ** (-jnp.arange(0, QK_ROPE_DIM, 2, dtype=F32) / QK_ROPE_DIM)
    z32 = jnp.zeros((ROPE_HALF,), F32)
    o32 = jnp.ones((ROPE_HALF,), F32)
    freq_tab = jnp.concatenate([inv_freq, z32, inv_freq, z32]).reshape(LANES, 1)
    sgn_tab = jnp.concatenate([-o32, z32, o32, z32]).reshape(LANES, 1)

    lane_pad = lambda v: jnp.pad(v.astype(F32), (0, LANES - n_ssd_heads)).reshape(1, LANES)
    a_neg = lane_pad(-jnp.exp(p["a_log"].astype(F32)) * LOG2_E)
    dt_bias = lane_pad(p["dt_bias"])
    d_skip_x = jnp.repeat(p["d_skip"].astype(F32), SSD_HEAD_DIM).reshape(1, d_inner)
    tri = jnp.tril(jnp.ones((T, T), BF16))
    expand = (jnp.arange(LANES)[:, None] == (jnp.arange(d_inner)[None, :] // SSD_HEAD_DIM)).astype(BF16)
    expand2 = jnp.concatenate([expand, expand], axis=0)

    tm = _tile(s, 512)
    assert tm % T == 0
    n_tiles = s // tm
    v_rows = V_HEAD_DIM + ONES_ROWS
    row_spec = lambda n: pl.BlockSpec((tm, n), lambda i: (i, 0))
    col_spec = lambda n: pl.BlockSpec((1, n, tm), lambda i: (i, 0, 0))
    q_t, kc, v_t, ssm = pl.pallas_call(
        functools.partial(_in_ssd_kernel, n_heads=n_mla_heads, q_scale=float(QK_HEAD_DIM) ** -0.5 * LOG2_E,
                          seq_tiles=n_tiles, d_inner=d_inner, n_ssd_heads=n_ssd_heads),
        grid=(m // tm,),
        in_specs=[row_spec(d_model), pl.BlockSpec((1, tm), lambda i: (0, i)), _resident((1, d_model)),
                  _resident((Q_LORA_RANK + KV_LORA_RANK, d_model)), _resident((d_inner, d_model)),
                  _resident((d_xbc, d_model)), _resident((2 * LANES, d_model)),
                  _resident((1, Q_LORA_RANK)), _resident((n_mla_heads * QK_PAD, Q_LORA_RANK)),
                  _resident((1, KV_LORA_RANK)), _resident((KV_LORA_RANK, n_mla_heads * QK_NOPE_DIM)),
                  _resident((mla_width, KV_LORA_RANK)),
                  _resident((LANES, 1)), _resident((LANES, 1)),
                  _resident((SSD_CONV, d_xbc)), _resident((1, d_xbc)),
                  _resident((1, LANES)), _resident((1, LANES)), _resident((1, d_inner)), _resident((1, d_inner)),
                  _resident((T, T)), _resident((2 * LANES, d_inner))],
        out_specs=[col_spec(n_mla_heads * QK_PAD), row_spec(n_mla_heads * QK_PAD), col_spec(n_mla_heads * v_rows),
                   row_spec(d_inner)],
        out_shape=[jax.ShapeDtypeStruct((m // tm, n_mla_heads * QK_PAD, tm), BF16),
                   jax.ShapeDtypeStruct((m, n_mla_heads * QK_PAD), BF16),
                   jax.ShapeDtypeStruct((m // tm, n_mla_heads * v_rows, tm), BF16),
                   jax.ShapeDtypeStruct((m, d_inner), BF16)],
        scratch_shapes=[pltpu.VMEM((CONV_HALO + tm, d_xbc), F32),
                        pltpu.VMEM((tm // T, 3 * T, 2 * LANES), BF16),
                        pltpu.VMEM((SSD_GROUPS, SSD_STATE, d_inner // SSD_GROUPS), F32)],
        compiler_params=pltpu.CompilerParams(dimension_semantics=("arbitrary",),
                                             vmem_limit_bytes=VMEM_LIMIT_BYTES),
        name="in_ssd",
    )(h.reshape(m, d_model), pos_row, _row(p["pre_mix_norm_w"]), w_lat, w_z, w_xbc,
      w_misc, _row(p["q_norm_w"]), w_uq_t, _row(p["kv_norm_w"]), w_kn, w_v_t, freq_tab, sgn_tab,
      p["conv_w"].astype(F32), _row(p["conv_b"]), dt_bias, a_neg, d_skip_x, _row(p["ssd_norm_w"]),
      tri, expand2)

    hp = next(c for c in (4, 2, 1) if n_mla_heads % c == 0)
    assert n_tiles % 2 == 0, "the attention kernel takes query tiles in pairs"
    n_wide = n_tiles // 2
    attn_t = pl.pallas_call(
        functools.partial(_mla_kernel, tile=tm, heads=hp),
        grid=(b, n_mla_heads // hp, n_wide),
        in_specs=[pl.BlockSpec((2, hp * QK_PAD, tm), lambda bi, hi, wi: (bi * n_wide + wi, hi, 0)),
                  pl.BlockSpec((1, s, hp * QK_PAD), lambda bi, hi, wi: (bi, 0, hi)),
                  pl.BlockSpec((1, n_tiles, hp * v_rows, tm), lambda bi, hi, wi: (bi, 0, hi, 0))],
        out_specs=pl.BlockSpec((2, hp * V_HEAD_DIM, tm), lambda bi, hi, wi: (bi * n_wide + wi, hi, 0)),
        out_shape=jax.ShapeDtypeStruct((m // tm, mla_width, tm), BF16),
        scratch_shapes=[pltpu.VMEM((2 * hp, tm, tm), F32), pltpu.VMEM((2 * hp, tm, tm), F32),
                        pltpu.VMEM((2 * hp, 1, tm), F32), pltpu.VMEM((2 * hp, v_rows, tm), F32)],
        compiler_params=pltpu.CompilerParams(dimension_semantics=("arbitrary", "arbitrary", "arbitrary"),
                                             vmem_limit_bytes=VMEM_LIMIT_BYTES),
        name="mla",
    )(q_t, kc.reshape(b, s, -1), v_t.reshape(b, n_tiles, n_mla_heads * v_rows, tm))

    w_out = p["w_out"].astype(BF16)
    orow = lambda n: pl.BlockSpec((tm, n), lambda i: (i, 0))
    h1 = pl.pallas_call(
        _out_proj_kernel,
        grid=(m // tm,),
        in_specs=[orow(d_model), col_spec(mla_width), orow(d_inner), _resident((mla_width, 1)),
                  _resident((mla_width, d_model)), _resident((d_inner, d_model)), _resident((1, d_model))],
        out_specs=orow(d_model),
        out_shape=jax.ShapeDtypeStruct((m, d_model), F32),
        compiler_params=pltpu.CompilerParams(dimension_semantics=("arbitrary",),
                                             vmem_limit_bytes=VMEM_LIMIT_BYTES),
        name="out_proj",
    )(h.reshape(m, d_model), attn_t, ssm, p["attn_out_norm_w"].astype(F32).reshape(mla_width, 1),
      w_out[:mla_width], w_out[mla_width:], _row(p["post_mix_norm_w"]))

    tf_m = _tile(m, 1024)
    tf_f = _tile(d_ff, 512)
    assert d_ff // tf_f >= 2, "the FFN kernel's first and last d_ff steps must be distinct"
    out = pl.pallas_call(
        _ffn_kernel,
        grid=(m // tf_m, d_ff // tf_f),
        in_specs=[pl.BlockSpec((tf_m, d_model), lambda i, f: (i, 0)),
                  _resident((1, d_model)),
                  pl.BlockSpec((d_model, tf_f), lambda i, f: (0, f)),
                  pl.BlockSpec((d_model, tf_f), lambda i, f: (0, f)),
                  pl.BlockSpec((tf_f, d_model), lambda i, f: (f, 0)),
                  _resident((1, d_model))],
        out_specs=pl.BlockSpec((tf_m, d_model), lambda i, f: (i, 0)),
        out_shape=jax.ShapeDtypeStruct((m, d_model), F32),
        scratch_shapes=[pltpu.VMEM((tf_m, d_model), BF16)],
        compiler_params=pltpu.CompilerParams(dimension_semantics=("arbitrary", "arbitrary"),
                                             vmem_limit_bytes=VMEM_LIMIT_BYTES),
        name="ffn",
    )(h1, _row(p["pre_ffn_norm_w"]), p["w_gate"].astype(BF16), p["w_up"].astype(BF16),
      p["w_down"], _row(p["post_ffn_norm_w"]))
    return out.reshape(b, s, d_model)


def kernel(x, positions, w_in, q_norm_w, w_uq, kv_norm_w, w_ukv, conv_w, conv_b, dt_bias, a_log, d_skip,
           ssd_norm_w, attn_out_norm_w, w_out, pre_mix_norm_w, post_mix_norm_w, pre_ffn_norm_w,
           post_ffn_norm_w, w_gate, w_up, w_down):
    stacked = dict(w_in=w_in, q_norm_w=q_norm_w, w_uq=w_uq, kv_norm_w=kv_norm_w, w_ukv=w_ukv, conv_w=conv_w,
                   conv_b=conv_b, dt_bias=dt_bias, a_log=a_log, d_skip=d_skip, ssd_norm_w=ssd_norm_w,
                   attn_out_norm_w=attn_out_norm_w, w_out=w_out, pre_mix_norm_w=pre_mix_norm_w,
                   post_mix_norm_w=post_mix_norm_w, pre_ffn_norm_w=pre_ffn_norm_w,
                   post_ffn_norm_w=post_ffn_norm_w, w_gate=w_gate, w_up=w_up, w_down=w_down)
    b, s, _ = x.shape
    pos_row = positions.astype(F32).reshape(1, b * s)
    h = x
    for l in range(w_in.shape[0]):
        h = _layer(h, pos_row, {k: v[l] for k, v in stacked.items()})
    return h
```

```python
import functools

import numpy as np
import jax
import jax.numpy as jnp
from jax import lax
from jax.experimental import pallas as pl
from jax.experimental.pallas import tpu as pltpu

F32 = jnp.float32
BF16 = jnp.bfloat16

V_HEAD_DIM = 128
QK_NOPE_DIM = 128
QK_ROPE_DIM = 64
QK_HEAD_DIM = QK_NOPE_DIM + QK_ROPE_DIM
Q_LORA_RANK = 512
KV_LORA_RANK = 512
ROPE_THETA = 10000.0
SSD_HEAD_DIM = 64
SSD_GROUPS = 2
SSD_STATE = 128
SSD_CONV = 4
SSD_CHUNK = 128
EPS = 1e-6

LANES = 128
SUBLANES = 8
VMEM_LIMIT_BYTES = 60 * 1024 * 1024

QK_PAD = 2 * LANES
ROPE_HALF = QK_ROPE_DIM // 2
LOG2_E = 1.4426950408889634
ONES_ROWS = 2 * SUBLANES
CONV_HALO = SUBLANES
WD_SLOTS = 3


def _rms(t, w):
    return t * lax.rsqrt(jnp.mean(t * t, axis=-1, keepdims=True) + EPS) * w


def _silu(t):
    h = 0.5 * t
    return h + h * jnp.tanh(h)


def _rope(t, cos, sin_signed):
    return t * cos + pltpu.roll(t, LANES // 2, axis=1) * sin_signed


def _ssd_chunk(k, ext_ref, hilo_ref, state_ref, z_c, dt_c, convw_ref, convb_ref, dtb_ref, aneg_ref,
               dskip_ref, nw_ref, tri_ref, e_ref, *, d_inner, n_heads):
    T, N, P, G = SSD_CHUNK, SSD_STATE, SSD_HEAD_DIM, SSD_GROUPS
    gw = d_inner // G

    window = ext_ref[k * T:k * T + CONV_HALO + T, :]
    conv = convb_ref[...] + convw_ref[SSD_CONV - 1:SSD_CONV, :] * window[CONV_HALO:]
    for tap in range(SSD_CONV - 1):
        shifted = pltpu.roll(window, SSD_CONV - 1 - tap, axis=0)[CONV_HALO:]
        conv = conv + convw_ref[tap:tap + 1, :] * shifted
    xbc = _silu(conv)
    xs = xbc[:, :d_inner]
    bm = xbc[:, d_inner:d_inner + G * N]
    cm = xbc[:, d_inner + G * N:]

    dt_in = dt_c + dtb_ref[...]
    dt = jnp.maximum(dt_in, 0.0) + jnp.log1p(jnp.exp(-jnp.abs(dt_in)))
    a = dt * aneg_ref[...]
    a_hi = a.astype(BF16)
    a_r = a - a_hi.astype(F32)
    a_mid = a_r.astype(BF16)
    a_lo = (a_r - a_mid.astype(F32)).astype(BF16)
    tri_b = tri_ref[...]
    a_cum = (jnp.dot(tri_b, a_hi, preferred_element_type=F32) + jnp.dot(tri_b, a_mid, preferred_element_type=F32)
             + jnp.dot(tri_b, a_lo, preferred_element_type=F32))
    a_cum_t = a_cum.T
    a_last = a_cum[T - 1:T, :]
    ea = jnp.exp2(a_cum)
    ds = jnp.exp2(a_last - a_cum)

    stacked = jnp.concatenate([dt, ea, ds], axis=0)
    s_hi = stacked.astype(BF16)
    hilo_ref[k, :, :LANES] = s_hi
    hilo_ref[k, :, LANES:] = (stacked - s_hi.astype(F32)).astype(BF16)
    expanded = jnp.dot(hilo_ref[k], e_ref[...], preferred_element_type=F32)
    dt_x, ea_x, ds_x = expanded[0:T], expanded[T:2 * T], expanded[2 * T:3 * T]

    xdt = xs * dt_x
    xdt_b = xdt.astype(BF16)
    xw_b = (xdt * ds_x).astype(BF16)

    row = lax.broadcasted_iota(jnp.int32, (T, T), 0)
    col = lax.broadcasted_iota(jnp.int32, (T, T), 1)
    tri = col <= row
    lane = lax.broadcasted_iota(jnp.int32, (T, 2 * P), 1)
    heads_per_group = n_heads // G

    y_parts = []
    y_off_parts = []
    for g in range(G):
        cols = slice(g * gw, (g + 1) * gw)
        bm_g = bm[:, g * N:(g + 1) * N]
        cm_g = cm[:, g * N:(g + 1) * N].astype(BF16)
        cb = lax.dot_general(cm_g, bm_g.astype(BF16), (((1,), (1,)), ((), ())), preferred_element_type=F32)
        for j in range(heads_per_group // 2):
            c0 = g * gw + 2 * j * P
            xp = xdt_b[:, c0:c0 + 2 * P]
            zero = jnp.zeros_like(xp)
            part = None
            for i, x_half in enumerate((jnp.where(lane < P, xp, zero), jnp.where(lane >= P, xp, zero))):
                h = g * heads_per_group + 2 * j + i
                seg = a_cum[:, h:h + 1] - a_cum_t[h:h + 1, :]
                w = (cb * jnp.exp2(jnp.where(tri, seg, -jnp.inf))).astype(BF16)
                d = jnp.dot(w, x_half, preferred_element_type=F32)
                part = d if part is None else part + d
            y_parts.append(part)
        prev = state_ref[g]
        y_off_parts.append(jnp.dot(cm_g, prev.astype(BF16), preferred_element_type=F32))
        new = jnp.dot(bm_g.T.astype(BF16), xw_b[:, cols], preferred_element_type=F32)
        state_ref[g] = prev * ea_x[T - 1:T, cols] + new
    y_diag = jnp.concatenate(y_parts, axis=1)
    y_off = jnp.concatenate(y_off_parts, axis=1) * ea_x

    y = y_diag + y_off + xs * dskip_ref[...]
    gated = y * _silu(z_c)
    outs = []
    for g in range(G):
        gg = gated[:, g * gw:(g + 1) * gw]
        outs.append(gg * lax.rsqrt(jnp.mean(gg * gg, axis=-1, keepdims=True) + EPS))
    return (jnp.concatenate(outs, axis=1) * nw_ref[...]).astype(BF16)


def _in_ssd_kernel(x_ref, pos_ref, wpre_ref, wlat_ref, wz_ref, wxbc_ref, wmisc_ref, qnw_ref, wuqt_ref, kvnw_ref,
                   wkn_ref, wvt_ref, freq_ref, sgn_ref, convw_ref, convb_ref, dtb_ref, aneg_ref, dskip_ref,
                   nw_ref, tri_ref, e_ref, qt_ref, kc_ref, vt_ref, ssm_ref, ext_ref, hilo_ref, state_ref,
                   *, n_heads, q_scale, seq_tiles, d_inner, n_ssd_heads):
    T = SSD_CHUNK
    tm = x_ref.shape[0]

    @pl.when(pl.program_id(0) % seq_tiles == 0)
    def _():
        ext_ref[0:CONV_HALO, :] = jnp.zeros((CONV_HALO, ext_ref.shape[1]), F32)
        state_ref[...] = jnp.zeros_like(state_ref)

    u = _rms(x_ref[...], wpre_ref[...]).astype(BF16)

    nt_dims = (((1,), (1,)), ((), ()))
    ext_ref[CONV_HALO:, :] = lax.dot_general(u, wxbc_ref[...], nt_dims, preferred_element_type=F32)
    z = lax.dot_general(u, wz_ref[...], nt_dims, preferred_element_type=F32)
    misc = lax.dot_general(u, wmisc_ref[...], nt_dims, preferred_element_type=F32)
    dt_raw = misc[:, LANES:]
    for k in range(tm // T):
        ssm_ref[k * T:(k + 1) * T, :] = _ssd_chunk(
            k, ext_ref, hilo_ref, state_ref, z[k * T:(k + 1) * T], dt_raw[k * T:(k + 1) * T],
            convw_ref, convb_ref, dtb_ref, aneg_ref, dskip_ref, nw_ref, tri_ref, e_ref,
            d_inner=d_inner, n_heads=n_ssd_heads)
    ext_ref[0:CONV_HALO, :] = ext_ref[tm:tm + CONV_HALO, :]

    ang_t = freq_ref[...] * pos_ref[...]
    cos_t = jnp.cos(ang_t)
    sin_t = jnp.sin(ang_t) * sgn_ref[...]
    k_rope = _rope(misc[:, :LANES], cos_t.T, sin_t.T).astype(BF16)

    cqkv = lax.dot_general(u, wlat_ref[...], nt_dims, preferred_element_type=F32)
    cqn = _rms(cqkv[:, :Q_LORA_RANK], qnw_ref[...]).astype(BF16)
    ckvn = _rms(cqkv[:, Q_LORA_RANK:], kvnw_ref[...]).astype(BF16)
    q_t = lax.dot_general(wuqt_ref[...], cqn, nt_dims, preferred_element_type=F32)
    cos_q = cos_t * q_scale
    sin_q = sin_t * q_scale
    half = LANES // 2
    for h in range(n_heads):
        base = h * QK_PAD
        qt_ref[0, base:base + LANES, :] = (q_t[base:base + LANES] * q_scale).astype(BF16)
        blk = q_t[base + LANES:base + QK_PAD]
        swapped = jnp.concatenate([blk[half:], blk[:half]], axis=0)
        qt_ref[0, base + LANES:base + QK_PAD, :] = (blk * cos_q + swapped * sin_q).astype(BF16)

    k_nope = jnp.dot(ckvn, wkn_ref[...], preferred_element_type=F32).astype(BF16)
    v_t = lax.dot_general(wvt_ref[...], ckvn, nt_dims, preferred_element_type=F32)
    v_rows = V_HEAD_DIM + ONES_ROWS
    for h in range(n_heads):
        kc_ref[:, h * QK_PAD:h * QK_PAD + LANES] = k_nope[:, h * LANES:(h + 1) * LANES]
        kc_ref[:, h * QK_PAD + LANES:(h + 1) * QK_PAD] = k_rope
        vt_ref[0, h * v_rows:h * v_rows + V_HEAD_DIM, :] = v_t[h * V_HEAD_DIM:(h + 1) * V_HEAD_DIM].astype(BF16)
        vt_ref[0, h * v_rows + V_HEAD_DIM:(h + 1) * v_rows, :] = jnp.ones((ONES_ROWS, tm), BF16)


def _mla_kernel(qt_ref, kc_ref, vt_ref, o_ref, sa_ref, sb_ref, m_ref, acc_ref, *, tile, heads):
    w = pl.program_id(2)
    v_rows = V_HEAD_DIM + ONES_ROWS
    chains = [(h, r) for h in range(heads) for r in range(2)]

    m_ref[...] = jnp.full_like(m_ref, -jnp.inf)
    acc_ref[...] = jnp.zeros_like(acc_ref)

    def scores(j, s_ref, only_r=None):
        rows = pl.ds(pl.multiple_of(j * tile, tile), tile)
        for c, (h, r) in enumerate(chains):
            if only_r is None or r == only_r:
                s_ref[c] = jnp.dot(kc_ref[0, rows, h * QK_PAD:(h + 1) * QK_PAD],
                                   qt_ref[r, h * QK_PAD:(h + 1) * QK_PAD, :], preferred_element_type=F32)

    def consume(j, s_ref, diag_r=None, only_r=None):
        for c, (h, r) in enumerate(chains):
            if only_r is not None and r != only_r:
                continue
            s = s_ref[c]
            if r == diag_r:
                k_idx = lax.broadcasted_iota(jnp.int32, s.shape, 0)
                q_idx = lax.broadcasted_iota(jnp.int32, s.shape, 1)
                s = jnp.where(k_idx <= q_idx, s, -jnp.inf)
            m_prev = m_ref[c]
            m_new = jnp.maximum(m_prev, jnp.max(s, axis=0, keepdims=True))
            alpha = jnp.exp2(m_prev - m_new)
            p = jnp.exp2(s - m_new)
            acc_ref[c] = alpha * acc_ref[c] + jnp.dot(vt_ref[0, j, h * v_rows:(h + 1) * v_rows, :], p.astype(BF16),
                                                      preferred_element_type=F32)
            m_ref[c] = m_new

    def pair(jj, carry):
        j = 2 * jj
        scores(j + 1, sb_ref)
        consume(j, sa_ref)
        scores(j + 2, sa_ref)
        consume(j + 1, sb_ref)
        return carry

    scores(0, sa_ref)
    lax.fori_loop(0, w, pair, 0)
    scores(2 * w + 1, sb_ref, only_r=1)
    consume(2 * w, sa_ref, diag_r=0)
    consume(2 * w + 1, sb_ref, diag_r=1, only_r=1)

    for c, (h, r) in enumerate(chains):
        out = acc_ref[c, :V_HEAD_DIM] / acc_ref[c, V_HEAD_DIM:V_HEAD_DIM + 1]
        o_ref[r, h * V_HEAD_DIM:(h + 1) * V_HEAD_DIM, :] = out.astype(o_ref.dtype)


def _out_proj_kernel(x_ref, attn_ref, ssm_ref, anw_ref, woa_ref, wos_ref, postw_ref, h_ref):
    a_t = attn_ref[0].astype(F32)
    scale = lax.rsqrt(jnp.mean(a_t * a_t, axis=0, keepdims=True) + EPS)
    attn_n = (a_t * scale * anw_ref[...]).astype(BF16).T
    mix = jnp.dot(attn_n, woa_ref[...], preferred_element_type=F32)
    mix = mix + jnp.dot(ssm_ref[...], wos_ref[...], preferred_element_type=F32)
    h_ref[...] = x_ref[...] + _rms(mix, postw_ref[...])


def _ffn_kernel(h_ref, prew_ref, wg_ref, wu_ref, wd_hbm, postw_ref, o_ref, v_ref, wd_ring, wd_sem):
    f = pl.program_id(1)
    n_f = pl.num_programs(1)
    last = n_f - 1
    tm = h_ref.shape[0]
    tf = wg_ref.shape[1]
    row_halves = (slice(0, tm // 2), slice(tm // 2, tm))
    step = pl.program_id(0) * n_f + f
    n_steps = pl.num_programs(0) * n_f

    def wd_copy(s):
        rows = pl.ds(pl.multiple_of((s % n_f) * tf, tf), tf)
        return pltpu.make_async_copy(wd_hbm.at[rows, :], wd_ring.at[s % WD_SLOTS], wd_sem.at[s % WD_SLOTS])

    @pl.when(step == 0)
    def _():
        for s in range(WD_SLOTS - 1):
            wd_copy(s).start()

    @pl.when(step + (WD_SLOTS - 1) < n_steps)
    def _():
        wd_copy(step + (WD_SLOTS - 1)).start()

    wd_copy(step).wait()
    wd_ref = wd_ring.at[step % WD_SLOTS]

    def prenorm(rows):
        v_ref[rows, :] = _rms(h_ref[rows, :], prew_ref[...]).astype(BF16)
        o_ref[rows, :] = jnp.zeros((tm // 2, o_ref.shape[1]), F32)

    def swiglu(rows):
        v = v_ref[rows, :]
        half = wg_ref.shape[1] // 2
        acts = []
        for c in range(2):
            cols = slice(c * half, (c + 1) * half)
            gate = jnp.dot(v, wg_ref[:, cols], preferred_element_type=F32)
            up = jnp.dot(v, wu_ref[:, cols], preferred_element_type=F32)
            acts.append((_silu(gate) * up).astype(BF16))
        o_ref[rows, :] += jnp.dot(jnp.concatenate(acts, axis=1), wd_ref[...], preferred_element_type=F32)

    def finalize(rows):
        o_ref[rows, :] = h_ref[rows, :] + _rms(o_ref[rows, :], postw_ref[...])

    @pl.when(f == 0)
    def _():
        prenorm(row_halves[0])
        swiglu(row_halves[0])
        prenorm(row_halves[1])
        swiglu(row_halves[1])

    @pl.when((f > 0) & (f < last))
    def _():
        swiglu(slice(0, tm))

    @pl.when(f == last)
    def _():
        swiglu(row_halves[0])
        finalize(row_halves[0])
        swiglu(row_halves[1])
        finalize(row_halves[1])


def _tile(n, want):
    t = min(n, want)
    assert n % t == 0, (n, t)
    return t


def _resident(shape):
    return pl.BlockSpec(shape, lambda *_: (0,) * len(shape), pipeline_mode=pl.Buffered(1))


def _row(w):
    return w.reshape(1, -1).astype(F32)


def _layer(h, pos_row, p):
    b, s, d_model = h.shape
    m = b * s
    n_mla_heads = p["w_uq"].shape[1] // QK_HEAD_DIM
    mla_width = n_mla_heads * V_HEAD_DIM
    d_inner = p["ssd_norm_w"].shape[0]
    n_ssd_heads = p["dt_bias"].shape[0]
    d_xbc = d_inner + 2 * SSD_GROUPS * SSD_STATE
    d_ff = p["w_gate"].shape[1]
    T = SSD_CHUNK
    assert n_ssd_heads * SSD_HEAD_DIM == d_inner and n_ssd_heads <= LANES

    w_in = p["w_in"].T.astype(BF16)
    offs = np.cumsum([0, Q_LORA_RANK, KV_LORA_RANK, QK_ROPE_DIM, d_inner, d_xbc, n_ssd_heads])
    w_cq, w_ckv, w_kr, w_z, w_xbc, w_dt = [w_in[offs[i]:offs[i + 1]] for i in range(6)]
    zpad = lambda n: jnp.zeros((n, d_model), BF16)
    w_lat = w_in[:offs[2]]
    w_misc = jnp.concatenate([w_kr[:ROPE_HALF], zpad(ROPE_HALF), w_kr[ROPE_HALF:], zpad(ROPE_HALF),
                              w_dt, zpad(LANES - n_ssd_heads)], axis=0)

    wq3 = p["w_uq"].reshape(Q_LORA_RANK, n_mla_heads, QK_HEAD_DIM)
    zq = jnp.zeros((Q_LORA_RANK, n_mla_heads, ROPE_HALF), wq3.dtype)
    w_uq_r = jnp.concatenate([
        wq3[..., :QK_NOPE_DIM], wq3[..., QK_NOPE_DIM:QK_NOPE_DIM + ROPE_HALF], zq,
        wq3[..., QK_NOPE_DIM + ROPE_HALF:], zq], axis=-1).reshape(Q_LORA_RANK, n_mla_heads * QK_PAD).astype(BF16)
    w_uq_t = w_uq_r.T
    wkv3 = p["w_ukv"].astype(BF16).reshape(KV_LORA_RANK, n_mla_heads, QK_NOPE_DIM + V_HEAD_DIM)
    w_kn = wkv3[..., :QK_NOPE_DIM].reshape(KV_LORA_RANK, n_mla_heads * QK_NOPE_DIM)
    w_v_t = wkv3[..., QK_NOPE_DIM:].reshape(KV_LORA_RANK, mla_width).T

    inv_freq = ROPE_THETA ** (-jnp.arange(0, QK_ROPE_DIM, 2, dtype=F32) / QK_ROPE_DIM)
    z32 = jnp.zeros((ROPE_HALF,), F32)
    o32 = jnp.ones((ROPE_HALF,), F32)
    freq_tab = jnp.concatenate([inv_freq, z32, inv_freq, z32]).reshape(LANES, 1)
    sgn_tab = jnp.concatenate([-o32, z32, o32, z32]).reshape(LANES, 1)

    lane_pad = lambda v: jnp.pad(v.astype(F32), (0, LANES - n_ssd_heads)).reshape(1, LANES)
    a_neg = lane_pad(-jnp.exp(p["a_log"].astype(F32)) * LOG2_E)
    dt_bias = lane_pad(p["dt_bias"])
    d_skip_x = jnp.repeat(p["d_skip"].astype(F32), SSD_HEAD_DIM).reshape(1, d_inner)
    tri = jnp.tril(jnp.ones((T, T), BF16))
    expand = (jnp.arange(LANES)[:, None] == (jnp.arange(d_inner)[None, :] // SSD_HEAD_DIM)).astype(BF16)
    expand2 = jnp.concatenate([expand, expand], axis=0)

    tm = _tile(s, 512)
    assert tm % T == 0
    n_tiles = s // tm
    v_rows = V_HEAD_DIM + ONES_ROWS
    row_spec = lambda n: pl.BlockSpec((tm, n), lambda i: (i, 0))
    col_spec = lambda n: pl.BlockSpec((1, n, tm), lambda i: (i, 0, 0))
    q_t, kc, v_t, ssm = pl.pallas_call(
        functools.partial(_in_ssd_kernel, n_heads=n_mla_heads, q_scale=float(QK_HEAD_DIM) ** -0.5 * LOG2_E,
                          seq_tiles=n_tiles, d_inner=d_inner, n_ssd_heads=n_ssd_heads),
        grid=(m // tm,),
        in_specs=[row_spec(d_model), pl.BlockSpec((1, tm), lambda i: (0, i)), _resident((1, d_model)),
                  _resident((Q_LORA_RANK + KV_LORA_RANK, d_model)), _resident((d_inner, d_model)),
                  _resident((d_xbc, d_model)), _resident((2 * LANES, d_model)),
                  _resident((1, Q_LORA_RANK)), _resident((n_mla_heads * QK_PAD, Q_LORA_RANK)),
                  _resident((1, KV_LORA_RANK)), _resident((KV_LORA_RANK, n_mla_heads * QK_NOPE_DIM)),
                  _resident((mla_width, KV_LORA_RANK)),
                  _resident((LANES, 1)), _resident((LANES, 1)),
                  _resident((SSD_CONV, d_xbc)), _resident((1, d_xbc)),
                  _resident((1, LANES)), _resident((1, LANES)), _resident((1, d_inner)), _resident((1, d_inner)),
                  _resident((T, T)), _resident((2 * LANES, d_inner))],
        out_specs=[col_spec(n_mla_heads * QK_PAD), row_spec(n_mla_heads * QK_PAD), col_spec(n_mla_heads * v_rows),
                   row_spec(d_inner)],
        out_shape=[jax.ShapeDtypeStruct((m // tm, n_mla_heads * QK_PAD, tm), BF16),
                   jax.ShapeDtypeStruct((m, n_mla_heads * QK_PAD), BF16),
                   jax.ShapeDtypeStruct((m // tm, n_mla_heads * v_rows, tm), BF16),
                   jax.ShapeDtypeStruct((m, d_inner), BF16)],
        scratch_shapes=[pltpu.VMEM((CONV_HALO + tm, d_xbc), F32),
                        pltpu.VMEM((tm // T, 3 * T, 2 * LANES), BF16),
                        pltpu.VMEM((SSD_GROUPS, SSD_STATE, d_inner // SSD_GROUPS), F32)],
        compiler_params=pltpu.CompilerParams(dimension_semantics=("arbitrary",),
                                             vmem_limit_bytes=VMEM_LIMIT_BYTES),
        name="in_ssd",
    )(h.reshape(m, d_model), pos_row, _row(p["pre_mix_norm_w"]), w_lat, w_z, w_xbc,
      w_misc, _row(p["q_norm_w"]), w_uq_t, _row(p["kv_norm_w"]), w_kn, w_v_t, freq_tab, sgn_tab,
      p["conv_w"].astype(F32), _row(p["conv_b"]), dt_bias, a_neg, d_skip_x, _row(p["ssd_norm_w"]),
      tri, expand2)

    hp = next(c for c in (4, 2, 1) if n_mla_heads % c == 0)
    assert n_tiles % 2 == 0, "the attention kernel takes query tiles in pairs"
    n_wide = n_tiles // 2
    attn_t = pl.pallas_call(
        functools.partial(_mla_kernel, tile=tm, heads=hp),
        grid=(b, n_mla_heads // hp, n_wide),
        in_specs=[pl.BlockSpec((2, hp * QK_PAD, tm), lambda bi, hi, wi: (bi * n_wide + wi, hi, 0)),
                  pl.BlockSpec((1, s, hp * QK_PAD), lambda bi, hi, wi: (bi, 0, hi)),
                  pl.BlockSpec((1, n_tiles, hp * v_rows, tm), lambda bi, hi, wi: (bi, 0, hi, 0))],
        out_specs=pl.BlockSpec((2, hp * V_HEAD_DIM, tm), lambda bi, hi, wi: (bi * n_wide + wi, hi, 0)),
        out_shape=jax.ShapeDtypeStruct((m // tm, mla_width, tm), BF16),
        scratch_shapes=[pltpu.VMEM((2 * hp, tm, tm), F32), pltpu.VMEM((2 * hp, tm, tm), F32),
                        pltpu.VMEM((2 * hp, 1, tm), F32), pltpu.VMEM((2 * hp, v_rows, tm), F32)],
        compiler_params=pltpu.CompilerParams(dimension_semantics=("arbitrary", "arbitrary", "arbitrary"),
                                             vmem_limit_bytes=VMEM_LIMIT_BYTES),
        name="mla",
    )(q_t, kc.reshape(b, s, -1), v_t.reshape(b, n_tiles, n_mla_heads * v_rows, tm))

    w_out = p["w_out"].astype(BF16)
    orow = lambda n: pl.BlockSpec((tm, n), lambda i: (i, 0))
    h1 = pl.pallas_call(
        _out_proj_kernel,
        grid=(m // tm,),
        in_specs=[orow(d_model), col_spec(mla_width), orow(d_inner), _resident((mla_width, 1)),
                  _resident((mla_width, d_model)), _resident((d_inner, d_model)), _resident((1, d_model))],
        out_specs=orow(d_model),
        out_shape=jax.ShapeDtypeStruct((m, d_model), F32),
        compiler_params=pltpu.CompilerParams(dimension_semantics=("arbitrary",),
                                             vmem_limit_bytes=VMEM_LIMIT_BYTES),
        name="out_proj",
    )(h.reshape(m, d_model), attn_t, ssm, p["attn_out_norm_w"].astype(F32).reshape(mla_width, 1),
      w_out[:mla_width], w_out[mla_width:], _row(p["post_mix_norm_w"]))

    tf_m = _tile(m, 1024)
    tf_f = _tile(d_ff, 512)
    assert d_ff // tf_f >= 2, "the FFN kernel's first and last d_ff steps must be distinct"
    out = pl.pallas_call(
        _ffn_kernel,
        grid=(m // tf_m, d_ff // tf_f),
        in_specs=[pl.BlockSpec((tf_m, d_model), lambda i, f: (i, 0)),
                  _resident((1, d_model)),
                  pl.BlockSpec((d_model, tf_f), lambda i, f: (0, f)),
                  pl.BlockSpec((d_model, tf_f), lambda i, f: (0, f)),
                  pl.BlockSpec(memory_space=pl.ANY),
                  _resident((1, d_model))],
        out_specs=pl.BlockSpec((tf_m, d_model), lambda i, f: (i, 0)),
        out_shape=jax.ShapeDtypeStruct((m, d_model), F32),
        scratch_shapes=[pltpu.VMEM((tf_m, d_model), BF16), pltpu.VMEM((WD_SLOTS, tf_f, d_model), BF16),
                        pltpu.SemaphoreType.DMA((WD_SLOTS,))],
        compiler_params=pltpu.CompilerParams(dimension_semantics=("arbitrary", "arbitrary"),
                                             vmem_limit_bytes=VMEM_LIMIT_BYTES),
        name="ffn",
    )(h1, _row(p["pre_ffn_norm_w"]), p["w_gate"].astype(BF16), p["w_up"].astype(BF16),
      p["w_down"].astype(BF16), _row(p["post_ffn_norm_w"]))
    return out.reshape(b, s, d_model)


def kernel(x, positions, w_in, q_norm_w, w_uq, kv_norm_w, w_ukv, conv_w, conv_b, dt_bias, a_log, d_skip,
           ssd_norm_w, attn_out_norm_w, w_out, pre_mix_norm_w, post_mix_norm_w, pre_ffn_norm_w,
           post_ffn_norm_w, w_gate, w_up, w_down):
    stacked = dict(w_in=w_in, q_norm_w=q_norm_w, w_uq=w_uq, kv_norm_w=kv_norm_w, w_ukv=w_ukv, conv_w=conv_w,
                   conv_b=conv_b, dt_bias=dt_bias, a_log=a_log, d_skip=d_skip, ssd_norm_w=ssd_norm_w,
                   attn_out_norm_w=attn_out_norm_w, w_out=w_out, pre_mix_norm_w=pre_mix_norm_w,
                   post_mix_norm_w=post_mix_norm_w, pre_ffn_norm_w=pre_ffn_norm_w,
                   post_ffn_norm_w=post_ffn_norm_w, w_gate=w_gate, w_up=w_up, w_down=w_down)
    b, s, _ = x.shape
    pos_row = positions.astype(F32).reshape(1, b * s)
    h = x
    for l in range(w_in.shape[0]):
        h = _layer(h, pos_row, {k: v[l] for k, v in stacked.items()})
    return h
```
